```python
import jax, jax.numpy as jnp
from jax import lax
import numpy as np

D_MODEL = 1024
BATCH = 8
SEQ = 2048
DEPTH = 4
DEC_BATCH = 128
DEC_SEQ = 1
PAST_LEN = 2048
PAGE_SIZE = 128

N_HEADS = 16
HEAD_DIM = D_MODEL // N_HEADS
N_KV = 4
HEADS_PER_KV = N_HEADS // N_KV
ROPE_DIM = HEAD_DIM // 4
ROPE_THETA = 500000.0
CMP_BLOCK = 32
CMP_STRIDE = 16
CMP_HID = 2 * HEAD_DIM
SLC_BLOCK = 64
TOP_N = 8
WINDOW = 512
QBLOCK = 128
N_NSA_COLS = N_HEADS * HEAD_DIM + 6 * N_KV * HEAD_DIM + 3 * N_HEADS
SCALE = HEAD_DIM ** -0.5
CHUNK = 128
SG_WIDTH = D_MODEL
SG_GROUPS = 8
SG_GROUP_DIM = SG_WIDTH // SG_GROUPS
D_FF = 4 * D_MODEL
N_NSA_LAYERS = (DEPTH + 1) // 2
N_SG_LAYERS = DEPTH // 2
EPS = 1e-6
NEG = -1e30
FORCE = 1e6

kernel_name = 'hybrid_nsa_gmlp_decode_step'


def _rmsnorm(x, g):
    x32 = x.astype(jnp.float32)
    y = x32 * lax.rsqrt(jnp.mean(x32 * x32, axis=-1, keepdims=True) + EPS) * g.astype(jnp.float32)
    return y.astype(x.dtype)


def _layernorm(x, g, b):
    x32 = x.astype(jnp.float32)
    mu = jnp.mean(x32, axis=-1, keepdims=True)
    var = jnp.mean(jnp.square(x32 - mu), axis=-1, keepdims=True)
    y = (x32 - mu) * lax.rsqrt(var + EPS) * g.astype(jnp.float32) + b.astype(jnp.float32)
    return y.astype(x.dtype)


def _rope(x, pos):
    half = ROPE_DIM // 2
    inv = jnp.power(jnp.float32(ROPE_THETA), -jnp.arange(half, dtype=jnp.float32) / half)
    ang = pos.astype(jnp.float32)[:, None] * inv[None, :]
    cos = jnp.cos(ang)[:, None, :]
    sin = jnp.sin(ang)[:, None, :]
    x32 = x.astype(jnp.float32)
    x1, x2 = x32[..., :half], x32[..., half:ROPE_DIM]
    out = jnp.concatenate([x1 * cos - x2 * sin, x2 * cos + x1 * sin, x32[..., ROPE_DIM:]], axis=-1)
    return out.astype(x.dtype)


def _masked_softmax(s, mask):
    s = jnp.where(mask, s.astype(jnp.float32), NEG)
    return jnp.where(mask, jax.nn.softmax(s, axis=-1), 0.0)


def _compress(k_raw, v_raw, pos_emb, w1, w2):
    T = k_raw.shape[1]
    n_cmp = (T - CMP_BLOCK) // CMP_STRIDE + 1
    idx = np.arange(n_cmp)[:, None] * CMP_STRIDE + np.arange(CMP_BLOCK)[None, :]

    def phi(x, p, a, b):
        blk = x[:, idx] + p[None, None, :, None, :]
        hid = jax.nn.gelu(jnp.einsum('bnlgd,lde->bnge', blk, a))
        return jnp.einsum('bnge,ed->bngd', hid, b)

    kc = phi(k_raw, pos_emb[0], w1[0], w2[0])
    vc = phi(v_raw, pos_emb[1], w1[1], w2[1])
    cpos = jnp.asarray(np.arange(n_cmp) * CMP_STRIDE + CMP_BLOCK - 1, dtype=jnp.int32)
    return _rope(kc, cpos), vc, cpos


def _cmp_attend(q, kc, vc, qpos, cpos):
    s = jnp.einsum('bqgzd,bngd->bgzqn', q, kc) * SCALE
    p = _masked_softmax(s, cpos[None, :] <= qpos[:, None])
    o = jnp.einsum('bgzqn,bngd->bqgzd', p.astype(vc.dtype), vc)
    return o, p.sum(axis=2)


def _select_blocks(p_grp, qpos, n_slc):
    n_cmp = p_grp.shape[-1]
    ci = np.arange(n_cmp)[:, None] * CMP_STRIDE
    sj = np.arange(n_slc)[None, :] * SLC_BLOCK
    overlap = ((ci < sj + SLC_BLOCK) & (ci + CMP_BLOCK > sj)).astype(np.float32)
    imp = jnp.einsum('bgqn,nj->bgqj', p_grp, jnp.asarray(overlap))
    j = jnp.arange(n_slc, dtype=jnp.int32)[None, :]
    cur = (qpos // SLC_BLOCK)[:, None]
    allowed = j <= cur
    forced = (j == 0) | (j == cur) | (j == cur - 1)
    score = jnp.where(forced, FORCE, jnp.where(allowed, imp, NEG))
    vals, idx = lax.top_k(score, min(TOP_N, n_slc))
    return idx, vals > NEG / 2


def _to_blocks(x, n_slc):
    B, T = x.shape[:2]
    x = jnp.pad(x, ((0, 0), (0, n_slc * SLC_BLOCK - T), (0, 0), (0, 0)))
    return x.reshape(B, n_slc, SLC_BLOCK, N_KV, HEAD_DIM).transpose(0, 3, 1, 2, 4)


def _slc_attend(q, k_blk, v_blk, idx, valid, qpos):
    B, G, Tq = idx.shape[:3]
    bi = jnp.arange(B)[:, None, None, None]
    gi = jnp.arange(G)[None, :, None, None]
    kg = k_blk[bi, gi, idx]
    vg = v_blk[bi, gi, idx]
    s = jnp.einsum('bqgzd,bgqkld->bgzqkl', q, kg) * SCALE
    kpos = idx[..., None] * SLC_BLOCK + jnp.arange(SLC_BLOCK, dtype=jnp.int32)
    mask = valid[..., None] & (kpos <= qpos[None, None, :, None, None])
    shp = s.shape
    p = _masked_softmax(s.reshape(shp[:4] + (-1,)), mask.reshape(B, G, 1, Tq, -1)).reshape(shp)
    return jnp.einsum('bgzqkl,bgqkld->bqgzd', p.astype(vg.dtype), vg)


def _win_attend(q, k, v, qpos, kpos):
    s = jnp.einsum('bqgzd,bsgd->bgzqs', q, k) * SCALE
    d = qpos[:, None] - kpos[None, :]
    mask = (d >= 0) & (d < WINDOW) & (kpos[None, :] >= 0)
    p = _masked_softmax(s, mask)
    return jnp.einsum('bgzqs,bsgd->bqgzd', p.astype(v.dtype), v)


def _nsa_project(h, w_in, pos):
    B, T, _ = h.shape
    proj = h @ w_in
    nq = N_HEADS * HEAD_DIM
    nkv = 6 * N_KV * HEAD_DIM
    q = _rope(proj[..., :nq].reshape(B, T, N_HEADS, HEAD_DIM), pos)
    q = q.reshape(B, T, N_KV, HEADS_PER_KV, HEAD_DIM)
    kv = proj[..., nq:nq + nkv].reshape(B, T, 6, N_KV, HEAD_DIM)
    gates = proj[..., nq + nkv:].reshape(B, T, 3, N_HEADS)
    kv4 = jnp.stack([kv[:, :, 0], kv[:, :, 1], _rope(kv[:, :, 2], pos), kv[:, :, 3]], axis=2)
    kvw = jnp.stack([_rope(kv[:, :, 4], pos), kv[:, :, 5]], axis=2)
    return q, gates, kv4, kvw


def _nsa_combine(o_c, o_s, o_w, gates, b_gate, w_out):
    B, T = gates.shape[:2]
    g = jax.nn.sigmoid(gates + b_gate.reshape(3, N_HEADS)).reshape(B, T, 3, N_KV, HEADS_PER_KV, 1)
    o = g[:, :, 0] * o_c + g[:, :, 1] * o_s + g[:, :, 2] * o_w
    return o.reshape(B, T, N_HEADS * HEAD_DIM) @ w_out


def _nsa_prompt(h, w_in, b_gate, cmp_pos, cmp_w1, cmp_w2, w_out):
    B, T, _ = h.shape
    pos = jnp.arange(T, dtype=jnp.int32)
    q, gates, kv4, kvw = _nsa_project(h, w_in, pos)
    kc, vc, cpos = _compress(kv4[:, :, 0], kv4[:, :, 1], cmp_pos, cmp_w1, cmp_w2)
    o_c, p_grp = _cmp_attend(q, kc, vc, pos, cpos)
    n_slc = -(-T // SLC_BLOCK)
    idx, valid = _select_blocks(p_grp, pos, n_slc)
    k_blk = _to_blocks(kv4[:, :, 2], n_slc)
    v_blk = _to_blocks(kv4[:, :, 3], n_slc)
    pad = ((0, 0), (WINDOW, 0), (0, 0), (0, 0))
    kw = jnp.pad(kvw[:, :, 0], pad)
    vw = jnp.pad(kvw[:, :, 1], pad)

    def block(qi):
        start = qi * QBLOCK
        qb = lax.dynamic_slice_in_dim(q, start, QBLOCK, axis=1)
        qp = start + jnp.arange(QBLOCK, dtype=jnp.int32)
        ib = lax.dynamic_slice_in_dim(idx, start, QBLOCK, axis=2)
        vb = lax.dynamic_slice_in_dim(valid, start, QBLOCK, axis=2)
        o_s = _slc_attend(qb, k_blk, v_blk, ib, vb, qp)
        kwb = lax.dynamic_slice_in_dim(kw, start, WINDOW + QBLOCK, axis=1)
        vwb = lax.dynamic_slice_in_dim(vw, start, WINDOW + QBLOCK, axis=1)
        kp = start - WINDOW + jnp.arange(WINDOW + QBLOCK, dtype=jnp.int32)
        o_w = _win_attend(qb, kwb, vwb, qp, kp)
        return o_s, o_w

    o_s, o_w = lax.map(block, jnp.arange(T // QBLOCK, dtype=jnp.int32))
    o_s = jnp.moveaxis(o_s, 0, 1).reshape(B, T, N_KV, HEADS_PER_KV, HEAD_DIM)
    o_w = jnp.moveaxis(o_w, 0, 1).reshape(B, T, N_KV, HEADS_PER_KV, HEAD_DIM)
    y = _nsa_combine(o_c, o_s, o_w, gates, b_gate, w_out)
    win = min(WINDOW, T)
    return y, kv4, kvw[:, T - win:]


def _nsa_sample(h, cache_kv, win_state, page_table, w_in, b_gate, cmp_pos, cmp_w1, cmp_w2, w_out):
    B, Tq, _ = h.shape
    past = page_table.shape[1] * PAGE_SIZE
    pos = past + jnp.arange(Tq, dtype=jnp.int32)
    q, gates, kv4, kvw = _nsa_project(h, w_in, pos)
    hist = cache_kv[page_table].reshape(B, past, 4, N_KV, HEAD_DIM)
    full = jnp.concatenate([hist, kv4.astype(hist.dtype)], axis=1)
    kc, vc, cpos = _compress(full[:, :, 0], full[:, :, 1], cmp_pos, cmp_w1, cmp_w2)
    o_c, p_grp = _cmp_attend(q, kc, vc, pos, cpos)
    n_slc = -(-(past + Tq) // SLC_BLOCK)
    idx, valid = _select_blocks(p_grp, pos, n_slc)
    o_s = _slc_attend(q, _to_blocks(full[:, :, 2], n_slc), _to_blocks(full[:, :, 3], n_slc), idx, valid, pos)
    wfull = jnp.concatenate([win_state, kvw.astype(win_state.dtype)], axis=1)
    kp = past - win_state.shape[1] + jnp.arange(wfull.shape[1], dtype=jnp.int32)
    o_w = _win_attend(q, wfull[:, :, 0], wfull[:, :, 1], pos, kp)
    y = _nsa_combine(o_c, o_s, o_w, gates, b_gate, w_out)
    return y, kv4, wfull[:, Tq:]


def _sg_mixer(h, w_in, ln_g, ln_b, w_s, b_s, w_out):
    B, T, _ = h.shape
    z = jax.nn.gelu(h @ w_in)
    u, v = z[..., :SG_WIDTH], z[..., SG_WIDTH:]
    v = _layernorm(v, ln_g, ln_b)
    nc = -(-T // CHUNK)
    vc = jnp.pad(v, ((0, 0), (0, nc * CHUNK - T), (0, 0))).reshape(B, nc, CHUNK, SG_GROUPS, SG_GROUP_DIM)
    w = jnp.where(np.tril(np.ones((CHUNK, CHUNK), dtype=bool)), w_s, 0.0)
    s = jnp.einsum('gts,bcsgd->bctgd', w, vc) + b_s.T[None, None, :, :, None]
    s = s.reshape(B, nc * CHUNK, SG_WIDTH)[:, :T]
    return (u * s) @ w_out, v


def _ffn(h, w1, w2):
    return jnp.square(jax.nn.relu(h @ w1)) @ w2


def setup_inputs(seed: int = 0) -> dict:
    key = jax.random.key(seed)
    ks = jax.random.split(key, 24)
    f32 = jnp.float32

    def nrm(k, shape, scale):
        return jax.random.normal(k, shape, f32) * scale

    n_pages = PAST_LEN // PAGE_SIZE
    n_used = DEC_BATCH * n_pages
    n_pool = n_used + n_used // 4
    win_s = min(WINDOW, PAST_LEN)
    page_table = jax.random.permutation(ks[4], n_pool)[:n_used].reshape(DEC_BATCH, n_pages).astype(jnp.int32)
    return {
        'x_prompt': nrm(ks[0], (BATCH, SEQ, D_MODEL), 1.0),
        'x_sample': nrm(ks[1], (DEC_BATCH, DEC_SEQ, D_MODEL), 1.0),
        'cache_nsa_kv': nrm(ks[2], (N_NSA_LAYERS, n_pool, PAGE_SIZE, 4, N_KV, HEAD_DIM), 1.0),
        'state_nsa_win': nrm(ks[3], (N_NSA_LAYERS, DEC_BATCH, win_s, 2, N_KV, HEAD_DIM), 1.0),
        'page_table': page_table,
        'g_mix': 1.0 + nrm(ks[5], (DEPTH, D_MODEL), 0.02),
        'g_ffn': 1.0 + nrm(ks[6], (DEPTH, D_MODEL), 0.02),
        'g_final': 1.0 + nrm(ks[7], (D_MODEL,), 0.02),
        'nsa_w_in': nrm(ks[8], (N_NSA_LAYERS, D_MODEL, N_NSA_COLS), D_MODEL ** -0.5),
        'nsa_b_gate': nrm(ks[9], (N_NSA_LAYERS, 3 * N_HEADS), 0.1),
        'nsa_cmp_pos': nrm(ks[10], (N_NSA_LAYERS, 2, CMP_BLOCK, HEAD_DIM), 0.5),
        'nsa_cmp_w1': nrm(ks[11], (N_NSA_LAYERS, 2, CMP_BLOCK, HEAD_DIM, CMP_HID), (CMP_BLOCK * HEAD_DIM) ** -0.5),
        'nsa_cmp_w2': nrm(ks[12], (N_NSA_LAYERS, 2, CMP_HID, HEAD_DIM), CMP_HID ** -0.5),
        'nsa_w_out': nrm(ks[13], (N_NSA_LAYERS, N_HEADS * HEAD_DIM, D_MODEL), (N_HEADS * HEAD_DIM) ** -0.5),
        'sg_w_in': nrm(ks[14], (N_SG_LAYERS, D_MODEL, 2 * SG_WIDTH), D_MODEL ** -0.5),
        'sg_ln_g': 1.0 + nrm(ks[15], (N_SG_LAYERS, SG_WIDTH), 0.02),
        'sg_ln_b': nrm(ks[16], (N_SG_LAYERS, SG_WIDTH), 0.02),
        'sg_w_spatial': nrm(ks[17], (N_SG_LAYERS, SG_GROUPS, CHUNK, CHUNK), 0.5 * CHUNK ** -0.5),
        'sg_b_spatial': 1.0 + nrm(ks[18], (N_SG_LAYERS, SG_GROUPS, CHUNK), 0.02),
        'sg_w_out': nrm(ks[19], (N_SG_LAYERS, SG_WIDTH, D_MODEL), SG_WIDTH ** -0.5),
        'ffn_w1': nrm(ks[20], (DEPTH, D_MODEL, D_FF), D_MODEL ** -0.5),
        'ffn_w2': nrm(ks[21], (DEPTH, D_FF, D_MODEL), D_FF ** -0.5),
    }


def reference(x_prompt, x_sample, cache_nsa_kv, state_nsa_win, page_table, g_mix, g_ffn, g_final,
              nsa_w_in, nsa_b_gate, nsa_cmp_pos, nsa_cmp_w1, nsa_cmp_w2, nsa_w_out,
              sg_w_in, sg_ln_g, sg_ln_b, sg_w_spatial, sg_b_spatial, sg_w_out, ffn_w1, ffn_w2):
    xp, xs = x_prompt, x_sample
    kv_p, win_p, kv_s, win_s, v_s = [], [], [], [], []
    for i in range(DEPTH):
        j = i // 2
        hp = _rmsnorm(xp, g_mix[i])
        hs = _rmsnorm(xs, g_mix[i])
        if i % 2 == 0:
            shared = (nsa_w_in[j], nsa_b_gate[j], nsa_cmp_pos[j], nsa_cmp_w1[j], nsa_cmp_w2[j], nsa_w_out[j])
            yp, kvp, wp = _nsa_prompt(hp, *shared)
            ys, kvs, ws = _nsa_sample(hs, cache_nsa_kv[j], state_nsa_win[j], page_table, *shared)
            kv_p.append(kvp)
            win_p.append(wp)
            kv_s.append(kvs)
            win_s.append(ws)
        else:
            shared = (sg_w_in[j], sg_ln_g[j], sg_ln_b[j], sg_w_spatial[j], sg_b_spatial[j], sg_w_out[j])
            yp, _ = _sg_mixer(hp, *shared)
            ys, vs = _sg_mixer(hs, *shared)
            v_s.append(vs)
        xp = xp + yp
        xs = xs + ys
        xp = xp + _ffn(_rmsnorm(xp, g_ffn[i]), ffn_w1[i], ffn_w2[i])
        xs = xs + _ffn(_rmsnorm(xs, g_ffn[i]), ffn_w1[i], ffn_w2[i])
    y_prompt = _rmsnorm(xp, g_final)
    y_sample = _rmsnorm(xs, g_final)
    new_kv_prompt = jnp.stack(kv_p)
    new_win_prompt = jnp.stack(win_p)
    new_kv_sample = jnp.stack(kv_s)
    new_win_sample = jnp.stack(win_s)
    new_sg_v_sample = jnp.stack(v_s)
    return (y_prompt, y_sample, new_kv_prompt, new_win_prompt, new_kv_sample, new_win_sample, new_sg_v_sample)
```

```python
import functools

import numpy as np
import jax
import jax.numpy as jnp
from jax import lax
from jax.experimental import pallas as pl
from jax.experimental.pallas import tpu as pltpu

F32 = jnp.float32
BF16 = jnp.bfloat16

D_MODEL = 1024
N_HEADS = 16
HEAD_DIM = 64
N_KV = 4
HEADS_PER_KV = 4
ROPE_DIM = 16
ROPE_THETA = 500000.0
CMP_BLOCK = 32
CMP_STRIDE = 16
CMP_HID = 128
SLC_BLOCK = 64
TOP_N = 8
WINDOW = 512
PAGE_SIZE = 128
CHUNK = 128
SG_GROUPS = 8
D_FF = 4096
EPS = 1e-6
NEG = -1e30
FORCE = 1e6
SCALE = HEAD_DIM ** -0.5

LANES = 128
VMEM_LIMIT = 56 * 1024 * 1024


def _cparams(n_axes):
    return pltpu.CompilerParams(dimension_semantics=("arbitrary",) * n_axes,
                                vmem_limit_bytes=VMEM_LIMIT)


def _dot(a, b):
    return jnp.dot(a, b, preferred_element_type=F32)


def _dot_nt(a, b):
    return lax.dot_general(a, b, (((1,), (1,)), ((), ())), preferred_element_type=F32)


def _split_dot(a, b):
    hi = a.astype(BF16)
    lo = (a - hi.astype(F32)).astype(BF16)
    return _dot(hi, b) + _dot(lo, b)


def _rmsnorm(x, g):
    return x * lax.rsqrt(jnp.mean(x * x, axis=-1, keepdims=True) + EPS) * g


def _rope_slab(x, c, s1, s2):
    return x * c + pltpu.roll(x, LANES - 8, 1) * s1 + pltpu.roll(x, 8, 1) * s2


def _rope_tables(pos):
    half = ROPE_DIM // 2
    inv = jnp.power(jnp.float32(ROPE_THETA), -jnp.arange(half, dtype=F32) / half)
    ang = pos.astype(F32)[:, None] * inv[None, :]
    cos, sin = jnp.cos(ang), jnp.sin(ang)
    n = pos.shape[0]
    z = lambda w: jnp.zeros((n, w), F32)
    c = jnp.concatenate([cos, cos, jnp.ones((n, HEAD_DIM - ROPE_DIM), F32)], axis=1)
    s1 = jnp.concatenate([-sin, z(HEAD_DIM - half)], axis=1)
    s2 = jnp.concatenate([z(half), sin, z(HEAD_DIM - ROPE_DIM)], axis=1)
    t2 = lambda a: jnp.concatenate([a, a], axis=1)
    return t2(c), t2(s1), t2(s2)


def _proj_kernel(x_ref, g_ref, c_ref, s1_ref, s2_ref, wq_ref, wkv_ref, wkw_ref, wg_ref,
                 q_ref, kv_ref, kw_ref, gt_ref):
    xb = _rmsnorm(x_ref[...], g_ref[...]).astype(BF16)
    c, s1, s2 = c_ref[...], s1_ref[...], s2_ref[...]
    q = _dot(xb, wq_ref[...])
    for j in range(q.shape[1] // LANES):
        sl = slice(j * LANES, (j + 1) * LANES)
        q_ref[:, sl] = _rope_slab(q[:, sl], c, s1, s2)
    kv = _dot(xb, wkv_ref[...])
    kv_ref[:, 0:512] = kv[:, 0:512]
    for j in (4, 5):
        sl = slice(j * LANES, (j + 1) * LANES)
        kv_ref[:, sl] = _rope_slab(kv[:, sl], c, s1, s2)
    kv_ref[:, 768:1024] = kv[:, 768:1024]
    kw = _dot(xb, wkw_ref[...])
    for j in (0, 1):
        sl = slice(j * LANES, (j + 1) * LANES)
        kw_ref[:, sl] = _rope_slab(kw[:, sl], c, s1, s2)
    kw_ref[:, 256:512] = kw[:, 256:512]
    gt_ref[...] = _dot(xb, wg_ref[...])


def _proj(x, g, tabs, tab_period_blocks, wq, wkv, wkw, wg, tm):
    m = x.shape[0]
    nq = wq.shape[1]
    full = lambda a: pl.BlockSpec(a.shape, lambda i: (0,) * a.ndim, pipeline_mode=pl.Buffered(1))
    tab_spec = pl.BlockSpec((tm, LANES), lambda i: (i % tab_period_blocks, 0))
    row = lambda w: pl.BlockSpec((tm, w), lambda i: (i, 0))
    return pl.pallas_call(
        _proj_kernel,
        grid=(m // tm,),
        in_specs=[row(D_MODEL), full(g), tab_spec, tab_spec, tab_spec,
                  full(wq), full(wkv), full(wkw), full(wg)],
        out_specs=[row(nq), row(1024), row(512), row(LANES)],
        out_shape=[jax.ShapeDtypeStruct((m, nq), F32), jax.ShapeDtypeStruct((m, 1024), F32),
                   jax.ShapeDtypeStruct((m, 512), F32), jax.ShapeDtypeStruct((m, LANES), F32)],
        compiler_params=_cparams(1),
        name="nsa_proj",
    )(x, g, *tabs, wq, wkv, wkw, wg)


def _compress_slab(load, lhs_ref, wcat_ref, p_ref, w2_ref, kv):
    lane = lax.broadcasted_iota(jnp.int32, (128, LANES), 1)
    low = lane < 64
    for v in range(2):
        col = kv * 2 + v
        for lp in range(8):
            xe = load(2 * lp, col)
            xo = load(2 * lp + 1, col)
            re = pltpu.roll(xe, 64, 1)
            ro = pltpu.roll(xo, 64, 1)
            dst = slice(lp * LANES, (lp + 1) * LANES)
            lhs_ref[(2 * v) * 128:(2 * v + 1) * 128, dst] = jnp.where(low, xe, ro).astype(BF16)
            lhs_ref[(2 * v + 1) * 128:(2 * v + 2) * 128, dst] = jnp.where(low, re, xo).astype(BF16)
    w = wcat_ref[kv]
    c = _dot(lhs_ref[...], w)
    pb = _dot(p_ref[kv], w)
    bias = pb[0:1, 0:128] + pb[1:2, 128:256]
    hids = []
    for g in range(N_KV):
        lo = c[g * 128:(g + 1) * 128, 0:128]
        hi = c[g * 128:(g + 1) * 128, 128:256]
        pre = lo + pltpu.roll(hi, 127, 0) + bias
        hids.append(jax.nn.gelu(pre).astype(BF16))
    outs = []
    for pr in range(2):
        hc = jnp.concatenate([hids[2 * pr], hids[2 * pr + 1]], axis=1)
        outs.append(_dot(hc, w2_ref[kv]))
    return outs


def _cmp_prompt_kernel(x0_ref, x1_ref, x2_ref, x3_ref, wcat_ref, p_ref, w2_ref, c_ref, s1_ref, s2_ref,
                       kc_ref, vc_ref, lhs_ref):
    xs = (x0_ref, x1_ref, x2_ref, x3_ref)

    def load(l, cb):
        return xs[cb][pl.ds(l, 128, stride=CMP_STRIDE), :]
    ko = _compress_slab(load, lhs_ref, wcat_ref, p_ref, w2_ref, 0)
    vo = _compress_slab(load, lhs_ref, wcat_ref, p_ref, w2_ref, 1)
    for pr in range(2):
        kc_ref[pr] = _rope_slab(ko[pr], c_ref[...], s1_ref[...], s2_ref[...])
        vc_ref[pr] = vo[pr]


def _cmp_prompt(kv4, n_batch, seq, wcat, pflat, w2bd, ctabs):
    full = lambda a: pl.BlockSpec(a.shape, lambda b: (0,) * a.ndim)
    out = pl.BlockSpec((None, 2, 128, LANES), lambda b: (b, 0, 0, 0))
    return pl.pallas_call(
        _cmp_prompt_kernel,
        grid=(n_batch,),
        in_specs=[pl.BlockSpec((seq, LANES), functools.partial(lambda cb, b: (b, cb), cb))
                  for cb in range(4)] + [full(wcat), full(pflat), full(w2bd),
                  full(ctabs[0]), full(ctabs[1]), full(ctabs[2])],
        out_specs=[out, out],
        out_shape=[jax.ShapeDtypeStruct((n_batch, 2, 128, LANES), F32)] * 2,
        scratch_shapes=[pltpu.VMEM((512, 1024), BF16)],
        compiler_params=_cparams(1),
        name="nsa_cmp_prompt",
    )(kv4, kv4, kv4, kv4, wcat, pflat, w2bd, *ctabs)


QT = 128
ROWS = 2 * HEADS_PER_KV * QT
TK_SLC = 256
TK_WIN = 128


def _flash(qp, k_ref, v_ref, lo, hi, tk, bias_fn, m_ref, l_ref, acc_ref):
    m_ref[...] = jnp.full(m_ref.shape, NEG, F32)
    l_ref[...] = jnp.zeros(l_ref.shape, F32)
    acc_ref[...] = jnp.zeros(acc_ref.shape, F32)

    def body(kt, carry):
        k0 = pl.multiple_of(kt * tk, tk)
        kk = k_ref[pl.ds(k0, tk), :].astype(BF16)
        vv = v_ref[pl.ds(k0, tk), :].astype(BF16)
        s = _dot_nt(qp, kk) + bias_fn(k0)
        m_prev = m_ref[...]
        m_new = jnp.maximum(m_prev, jnp.max(s, axis=1, keepdims=True))
        alpha = jnp.exp(m_prev - m_new)
        p = jnp.exp(s - jnp.concatenate([m_new] * (tk // LANES), axis=1))
        l_ref[...] = alpha * l_ref[...] + jnp.sum(p, axis=1, keepdims=True)
        acc_ref[...] = alpha * acc_ref[...] + _dot(p.astype(BF16), vv)
        m_ref[...] = m_new
        return carry

    lax.fori_loop(lo, hi, body, 0)
    return acc_ref[...] / l_ref[...]


def _unstack_heads(x):
    lane = lax.broadcasted_iota(jnp.int32, (QT, LANES), 1)
    low = lane < 64
    cols = []
    for mpair in range(4):
        gi = mpair // 2
        a = x[(2 * mpair) * QT:(2 * mpair + 1) * QT]
        b = x[(2 * mpair + 1) * QT:(2 * mpair + 2) * QT]
        if gi == 0:
            b = pltpu.roll(b, 64, 1)
        else:
            a = pltpu.roll(a, 64, 1)
        cols.append(jnp.where(low, a, b))
    return jnp.concatenate(cols, axis=1)


def _attn_kernel(q_ref, ks_ref, vs_ref, kw_ref, vw_ref, kc_ref, vc_ref, gt_ref, bg_ref,
                 ov_ref, ex_ref, gx_ref, o_ref, m_ref, l_ref, acc_ref):
    qt = pl.program_id(2)
    lane = lax.broadcasted_iota(jnp.int32, (QT, LANES), 1)
    rowq = lax.broadcasted_iota(jnp.int32, (QT, LANES), 0)
    tq = qt * QT + rowq
    low = lane < 64

    blocks = []
    for gi in range(2):
        for z in range(HEADS_PER_KV):
            hh = gi * HEADS_PER_KV + z
            slab = q_ref[:, (hh // 2) * LANES:(hh // 2 + 1) * LANES]
            if (hh % 2) != gi:
                slab = pltpu.roll(slab, 64, 1)
            keep = low if gi == 0 else jnp.logical_not(low)
            blocks.append(jnp.where(keep, slab, 0.0) * SCALE)
    qp = jnp.concatenate(blocks, axis=0).astype(BF16)

    sc = _dot_nt(qp, kc_ref[...].astype(BF16))
    ok_c = (lane * CMP_STRIDE + (CMP_BLOCK - 1)) <= tq
    bias_c = jnp.where(ok_c, 0.0, NEG)
    sc = sc + jnp.concatenate([bias_c] * 8, axis=0)
    mc = jnp.max(sc, axis=1, keepdims=True)
    valid_c = jnp.concatenate([jnp.where(ok_c, 1.0, 0.0)] * 8, axis=0)
    ec = jnp.exp(sc - mc) * valid_c
    lc = jnp.sum(ec, axis=1, keepdims=True)
    pc = ec / jnp.where(lc > 0.0, lc, 1.0)
    o_c = _dot(pc.astype(BF16), vc_ref[...].astype(BF16))

    jf = lane.astype(F32)
    cur = jnp.right_shift(tq, 6)
    allowed = lane <= cur
    forced = (lane == 0) | (lane == cur) | (lane == cur - 1)
    sel_bf = []
    for gi in range(2):
        pg = pc[(gi * 4) * QT:(gi * 4 + 1) * QT]
        for z in range(1, HEADS_PER_KV):
            pg = pg + pc[(gi * 4 + z) * QT:(gi * 4 + z + 1) * QT]
        imp = _split_dot(pg, ov_ref[...])
        score = jnp.where(forced, FORCE, jnp.where(allowed, imp, NEG))
        sel = jnp.zeros((QT, LANES), F32)
        for _ in range(TOP_N):
            mx = jnp.max(score, axis=1, keepdims=True)
            first = jnp.min(jnp.where(score == mx, jf, 1e9), axis=1, keepdims=True)
            hit = jf == first
            sel = jnp.where(hit & (mx > NEG / 2), 1.0, sel)
            score = jnp.where(hit, -3e38, score)
        sel_bf.append(sel.astype(BF16))

    def bias_slc(k0):
        ex = ex_ref[:, pl.ds(k0, TK_SLC)]
        kpos = k0 + lax.broadcasted_iota(jnp.int32, (QT, TK_SLC), 1)
        tqk = qt * QT + lax.broadcasted_iota(jnp.int32, (QT, TK_SLC), 0)
        causal = kpos <= tqk
        parts = []
        for gi in range(2):
            mk = _dot(sel_bf[gi], ex)
            b = jnp.where((mk > 0.5) & causal, 0.0, NEG)
            parts += [b] * HEADS_PER_KV
        return jnp.concatenate(parts, axis=0)

    n_slc_tiles = (qt + 2) // 2
    o_s = _flash(qp, ks_ref, vs_ref, 0, n_slc_tiles, TK_SLC, bias_slc, m_ref, l_ref, acc_ref)

    def bias_win(k0):
        kpos = k0 + lax.broadcasted_iota(jnp.int32, (QT, TK_WIN), 1)
        tqk = qt * QT + lax.broadcasted_iota(jnp.int32, (QT, TK_WIN), 0)
        d = tqk - kpos
        b = jnp.where((d >= 0) & (d < WINDOW), 0.0, NEG)
        return jnp.concatenate([b] * 8, axis=0)

    lo_w = jnp.maximum(qt - WINDOW // TK_WIN, 0)
    o_w = _flash(qp, kw_ref, vw_ref, lo_w, qt + 1, TK_WIN, bias_win, m_ref, l_ref, acc_ref)

    gsig = jax.nn.sigmoid(gt_ref[...] + bg_ref[...])
    out = (_split_dot(gsig, gx_ref[0]) * _unstack_heads(o_c)
           + _split_dot(gsig, gx_ref[1]) * _unstack_heads(o_s)
           + _split_dot(gsig, gx_ref[2]) * _unstack_heads(o_w))
    o_ref[...] = out


def _attn_consts(seq):
    n = np.arange(LANES)
    ci = n[:, None] * CMP_STRIDE
    sj = n[None, :] * SLC_BLOCK
    n_cmp = (seq - CMP_BLOCK) // CMP_STRIDE + 1
    n_slc = -(-seq // SLC_BLOCK)
    ov = ((ci < sj + SLC_BLOCK) & (ci + CMP_BLOCK > sj) & (n[:, None] < n_cmp) & (n[None, :] < n_slc))
    ex = (np.arange(seq)[None, :] // SLC_BLOCK) == n[:, None]
    gx = np.zeros((2, 3, LANES, 512), np.float32)
    for gp in range(2):
        for c in range(3):
            for hh in range(8):
                gx[gp, c, c * N_HEADS + gp * 8 + hh, hh * 64:(hh + 1) * 64] = 1.0
    return (jnp.asarray(ov.astype(np.float32), BF16), jnp.asarray(ex.astype(np.float32), BF16),
            jnp.asarray(gx, BF16))


def _attn_prompt(q, kv4, kvw, kc, vc, gates, bgate, n_batch, seq):
    ov, ex, gx = _attn_consts(seq)
    nqt = seq // QT
    kspec = lambda col0: pl.BlockSpec((seq, LANES), lambda b, gp, t: (b, col0 + gp))
    cspec = pl.BlockSpec((None, None, 128, LANES), lambda b, gp, t: (b, gp, 0, 0))
    return pl.pallas_call(
        _attn_kernel,
        grid=(n_batch, 2, nqt),
        in_specs=[
            pl.BlockSpec((QT, 512), lambda b, gp, t: (b * nqt + t, gp)),
            kspec(4), kspec(6), kspec(0), kspec(2), cspec, cspec,
            pl.BlockSpec((QT, LANES), lambda b, gp, t: (b * nqt + t, 0)),
            pl.BlockSpec((1, LANES), lambda b, gp, t: (0, 0)),
            pl.BlockSpec(ov.shape, lambda b, gp, t: (0, 0)),
            pl.BlockSpec(ex.shape, lambda b, gp, t: (0, 0)),
            pl.BlockSpec((None, 3, LANES, 512), lambda b, gp, t: (gp, 0, 0, 0)),
        ],
        out_specs=pl.BlockSpec((QT, 512), lambda b, gp, t: (b * nqt + t, gp)),
        out_shape=jax.ShapeDtypeStruct((n_batch * seq, 1024), F32),
        scratch_shapes=[pltpu.VMEM((ROWS, LANES), F32)] * 3,
        compiler_params=_cparams(3),
        name="nsa_attn_prompt",
    )(q, kv4, kv4, kvw, kvw, kc, vc, gates, bgate, ov, ex, gx)


N_PAGES = 16


def _s1_kernel(layer, pt_ref, cache_ref, q_ref, wcat_ref, p_ref, w2_ref, c_ref, s1_ref, s2_ref,
               ov_ref, oc_ref, sel_ref, raw_ref, slab_ref, lhs_ref, sem_ref):
    b = pl.program_id(0)
    nb = pl.num_programs(0)

    def page_copy(bb, slot, p):
        return pltpu.make_async_copy(
            cache_ref.at[layer, pt_ref[bb, p], pl.ds(0, 2)],
            raw_ref.at[slot, p],
            sem_ref.at[slot])

    def start_all(bb, slot):
        for p in range(N_PAGES):
            page_copy(bb, slot, p).start()

    slot = b % 2

    @pl.when(b == 0)
    def _():
        start_all(0, 0)

    @pl.when(b + 1 < nb)
    def _():
        start_all(b + 1, 1 - slot)

    for p in range(N_PAGES):
        page_copy(b, slot, p).wait()

    for p in range(N_PAGES):
        for cb in range(4):
            kind, pr = cb // 2, cb % 2
            slab_ref[cb, p * PAGE_SIZE:(p + 1) * PAGE_SIZE, :] = (
                raw_ref[slot, p, kind, pr * LANES:(pr + 1) * LANES, :].T)

    def load(l, cb):
        return slab_ref[cb, pl.ds(l, 128, stride=CMP_STRIDE), :]

    ko = _compress_slab(load, lhs_ref, wcat_ref, p_ref, w2_ref, 0)
    vo = _compress_slab(load, lhs_ref, wcat_ref, p_ref, w2_ref, 1)
    kc = jnp.concatenate([_rope_slab(k, c_ref[...], s1_ref[...], s2_ref[...]) for k in ko], axis=1)
    vc = jnp.concatenate(vo, axis=1)

    qp = (q_ref[0] * SCALE).astype(BF16)
    sc = _dot_nt(qp, kc.astype(BF16))
    lane = lax.broadcasted_iota(jnp.int32, (N_HEADS, LANES), 1)
    ok = lane < (LANES - 1)
    sc = jnp.where(ok, sc, NEG)
    mc = jnp.max(sc, axis=1, keepdims=True)
    ec = jnp.where(ok, jnp.exp(sc - mc), 0.0)
    pc = ec / jnp.sum(ec, axis=1, keepdims=True)
    oc_ref[0] = _dot(pc.astype(BF16), vc.astype(BF16))

    imp_h = _split_dot(pc, ov_ref[...])
    rowg = jnp.right_shift(lax.broadcasted_iota(jnp.int32, (N_HEADS, LANES), 0), 2)
    row8 = lax.broadcasted_iota(jnp.int32, (8, LANES), 0)
    imp = jnp.zeros((8, LANES), F32)
    for g in range(N_KV):
        ig = jnp.sum(jnp.where(rowg == g, imp_h, 0.0), axis=0, keepdims=True)
        imp = jnp.where(row8 == g, ig, imp)
    lane8 = lax.broadcasted_iota(jnp.int32, (8, LANES), 1)
    jf = lane8.astype(F32)
    last = 2048 // SLC_BLOCK
    score = jnp.where((lane8 >= 1) & (lane8 <= last - 2), imp, NEG)
    picks = jnp.where(lane8 == 6, float(last - 1), 0.0)
    for k in range(TOP_N - 3):
        mx = jnp.max(score, axis=1, keepdims=True)
        first = jnp.min(jnp.where(score == mx, jf, 1e9), axis=1, keepdims=True)
        picks = jnp.where(lane8 == k, first, picks)
        score = jnp.where(jf == first, -3e38, score)
    sel_ref[0] = picks.astype(jnp.int32)


def _s1(layer, page_table, cache, q_exp, wcat, pflat, w2bd, ctabs, ov):
    nb = q_exp.shape[0]
    full = lambda a: pl.BlockSpec(a.shape, lambda b, pt: (0,) * a.ndim)
    grid_spec = pltpu.PrefetchScalarGridSpec(
        num_scalar_prefetch=1,
        grid=(nb,),
        in_specs=[pl.BlockSpec(memory_space=pl.ANY),
                  pl.BlockSpec((1, N_HEADS, 256), lambda b, pt: (b, 0, 0)),
                  full(wcat), full(pflat), full(w2bd), full(ctabs[0]), full(ctabs[1]), full(ctabs[2]),
                  full(ov)],
        out_specs=[pl.BlockSpec((1, N_HEADS, 256), lambda b, pt: (b, 0, 0)),
                   pl.BlockSpec((1, 8, LANES), lambda b, pt: (b, 0, 0))],
        scratch_shapes=[pltpu.VMEM((2, N_PAGES, 2, 256, PAGE_SIZE), F32),
                        pltpu.VMEM((4, N_PAGES * PAGE_SIZE, LANES), F32),
                        pltpu.VMEM((512, 1024), BF16),
                        pltpu.SemaphoreType.DMA((2,))],
    )
    return pl.pallas_call(
        functools.partial(_s1_kernel, layer),
        grid_spec=grid_spec,
        out_shape=[jax.ShapeDtypeStruct((nb, N_HEADS, 256), F32),
                   jax.ShapeDtypeStruct((nb, 8, LANES), jnp.int32)],
        compiler_params=_cparams(1),
        name="nsa_sample_s1",
    )(page_table, cache, q_exp, wcat, pflat, w2bd, *ctabs, ov)


N_HIST = TOP_N - 1
KSEL = N_HIST * PAGE_SIZE


def _s2_kernel(layer, has_prev, pt_ref, sel_ref, cache_ref, win_ref, q_ref, oc_ref, kvn_ref, kwn_ref,
               gt_ref, bg_ref, hsel_ref, *rest):
    o_ref, nwin_ref, kbuf_ref, vbuf_ref, sem_ref = rest[1:] if has_prev else rest
    if has_prev:
        nwin_ref[0:layer] = rest[0][...]
    b = pl.program_id(0)
    nb = pl.num_programs(0)

    def copies(bb, slot, g, k):
        blk = sel_ref[(bb * N_KV + g) * 8 + k]
        page = pt_ref[bb, jnp.right_shift(blk, 1)]
        rows = pl.ds(g * HEAD_DIM, HEAD_DIM)
        dst = pl.ds(k * PAGE_SIZE, PAGE_SIZE)
        return (pltpu.make_async_copy(cache_ref.at[layer, page, 2, rows, :],
                                      kbuf_ref.at[slot, g, :, dst], sem_ref.at[slot]),
                pltpu.make_async_copy(cache_ref.at[layer, page, 3, rows, :],
                                      vbuf_ref.at[slot, g, :, dst], sem_ref.at[slot]))

    def start_all(bb, slot):
        for g in range(N_KV):
            for k in range(N_HIST):
                ck, cv = copies(bb, slot, g, k)
                ck.start()
                cv.start()

    slot = b % 2

    @pl.when(b == 0)
    def _():
        start_all(0, 0)

    @pl.when(b + 1 < nb)
    def _():
        start_all(b + 1, 1 - slot)

    for g in range(N_KV):
        for k in range(N_HIST):
            ck, cv = copies(b, slot, g, k)
            ck.wait()
            cv.wait()

    qf = q_ref[0] * SCALE
    qp = qf.astype(BF16)
    q16 = (qf[:, 0:64] + qf[:, 64:128] + qf[:, 128:192] + qf[:, 192:256]).astype(BF16)
    lane_k = lax.broadcasted_iota(jnp.int32, (N_HEADS, KSEL), 1)
    rowg_k = jnp.right_shift(lax.broadcasted_iota(jnp.int32, (N_HEADS, KSEL), 0), 2)
    rowg_d = jnp.right_shift(lax.broadcasted_iota(jnp.int32, (N_HEADS, HEAD_DIM), 0), 2)
    tile_k = jnp.right_shift(lane_k, 7)
    half_k = jnp.bitwise_and(jnp.right_shift(lane_k, 6), 1)

    def attend(s_hist, s_new, pv_hist, v_new):
        m = jnp.maximum(jnp.max(s_hist, axis=1, keepdims=True), s_new)
        e = jnp.exp(s_hist - m)
        en = jnp.exp(s_new - m)
        l = jnp.sum(e, axis=1, keepdims=True) + en
        return (pv_hist(e.astype(BF16)) + en * v_new) / l

    kn = kvn_ref[0]
    s_hist = jnp.zeros((N_HEADS, KSEL), F32)
    v16 = jnp.zeros((N_HEADS, HEAD_DIM), F32)
    for g in range(N_KV):
        sg = _dot(q16, kbuf_ref[slot, g].astype(BF16))
        want = jnp.zeros((N_HEADS, KSEL), jnp.int32)
        for k in range(N_HIST):
            half = jnp.bitwise_and(sel_ref[(b * N_KV + g) * 8 + k], 1)
            want = jnp.where(tile_k == k, half, want)
        s_hist = jnp.where(rowg_k == g, jnp.where(half_k == want, sg, NEG), s_hist)
        v16 = jnp.where(rowg_d == g, kn[:, 768 + g * HEAD_DIM:768 + (g + 1) * HEAD_DIM], v16)
    s_new = jnp.sum(qf * kn[:, 512:768], axis=1, keepdims=True)

    def pv_slc(e):
        o = jnp.zeros((N_HEADS, HEAD_DIM), F32)
        for g in range(N_KV):
            o = jnp.where(rowg_d == g, _dot_nt(e, vbuf_ref[slot, g].astype(BF16)), o)
        return o

    o_s = attend(s_hist, s_new, pv_slc, v16)
    o_s = jnp.concatenate([o_s] * N_KV, axis=1)

    wn = kwn_ref[0]
    s_w = _dot(qp, win_ref[0].astype(BF16))
    colw = lax.broadcasted_iota(jnp.int32, (N_HEADS, WINDOW), 1)
    s_w = jnp.where(colw >= 1, s_w, NEG)
    s_wn = jnp.sum(qf * wn[:, 0:256], axis=1, keepdims=True)
    vw = win_ref[1].astype(BF16)
    o_w = attend(s_w, s_wn, lambda e: _dot_nt(e, vw), wn[:, 256:512])

    lane = lax.broadcasted_iota(jnp.int32, (LANES, LANES), 1)
    for c in range(2):
        for rs in range(2):
            rows = slice(rs * LANES, (rs + 1) * LANES)
            new = wn[:, c * 256 + rs * LANES:c * 256 + (rs + 1) * LANES]
            col = jnp.broadcast_to(new, (LANES, LANES)).T
            tiles = [pltpu.roll(win_ref[c, rows, j * LANES:(j + 1) * LANES], LANES - 1, 1)
                     for j in range(WINDOW // LANES)]
            tiles.append(col)
            for j in range(WINDOW // LANES):
                nwin_ref[layer, c, rows, j * LANES:(j + 1) * LANES] = jnp.where(
                    lane == LANES - 1, tiles[j + 1], tiles[j])

    gsig = jax.nn.sigmoid(gt_ref[0] + bg_ref[...])
    gcol = lambda c: jnp.sum(hsel_ref[c] * gsig, axis=1, keepdims=True)
    o_ref[0] = gcol(0) * oc_ref[0] + gcol(1) * o_s + gcol(2) * o_w


def _s2(layer, page_table, sel_flat, cache, win, q_exp, oc, kv4s, kvws, gates, bgate, hsel, prev):
    nb = q_exp.shape[0]
    hq = pl.BlockSpec((1, N_HEADS, 256), lambda b, pt, sl: (b, 0, 0))
    row = lambda w: pl.BlockSpec((1, 1, w), lambda b, pt, sl: (b, 0, 0))
    wspec = pl.BlockSpec((None, None, 2, 256, WINDOW), lambda b, pt, sl: (layer, b, 0, 0, 0))
    stack = lambda n: pl.BlockSpec((n, None, 2, 256, WINDOW), lambda b, pt, sl: (0, b, 0, 0, 0))
    in_specs = [pl.BlockSpec(memory_space=pl.ANY), wspec,
                hq, hq, row(1024), row(512), row(LANES),
                pl.BlockSpec((1, LANES), lambda b, pt, sl: (0, 0)),
                pl.BlockSpec(hsel.shape, lambda b, pt, sl: (0, 0, 0))]
    args = [page_table, sel_flat, cache, win, q_exp, oc, kv4s, kvws, gates, bgate, hsel]
    if prev is not None:
        in_specs.append(stack(layer))
        args.append(prev)
    grid_spec = pltpu.PrefetchScalarGridSpec(
        num_scalar_prefetch=2,
        grid=(nb,),
        in_specs=in_specs,
        out_specs=[hq, stack(layer + 1)],
        scratch_shapes=[pltpu.VMEM((2, N_KV, HEAD_DIM, KSEL), F32),
                        pltpu.VMEM((2, N_KV, HEAD_DIM, KSEL), F32),
                        pltpu.SemaphoreType.DMA((2,))],
    )
    return pl.pallas_call(
        functools.partial(_s2_kernel, layer, prev is not None),
        grid_spec=grid_spec,
        out_shape=[jax.ShapeDtypeStruct((nb, N_HEADS, 256), F32),
                   jax.ShapeDtypeStruct((layer + 1, nb, 2, 256, WINDOW), F32)],
        compiler_params=_cparams(1),
        name="nsa_sample_s2",
    )(*args)


def _sg_in_kernel(x_ref, g_ref, w_ref, lg_ref, lb_ref, u_ref, v_ref):
    xb = _rmsnorm(x_ref[...], g_ref[...]).astype(BF16)
    u_ref[...] = jax.nn.gelu(_dot(xb, w_ref[:, 0:1024]))
    v = jax.nn.gelu(_dot(xb, w_ref[:, 1024:2048]))
    mu = jnp.mean(v, axis=-1, keepdims=True)
    var = jnp.mean(jnp.square(v - mu), axis=-1, keepdims=True)
    v_ref[...] = (v - mu) * lax.rsqrt(var + EPS) * lg_ref[...] + lb_ref[...]


def _sg_in(x, g, w, lg, lb, tm):
    m = x.shape[0]
    full = lambda a: pl.BlockSpec(a.shape, lambda i: (0,) * a.ndim, pipeline_mode=pl.Buffered(1))
    row = pl.BlockSpec((tm, 1024), lambda i: (i, 0))
    return pl.pallas_call(
        _sg_in_kernel,
        grid=(m // tm,),
        in_specs=[row, full(g), full(w), full(lg), full(lb)],
        out_specs=[row, row],
        out_shape=[jax.ShapeDtypeStruct((m, 1024), F32)] * 2,
        compiler_params=_cparams(1),
        name="sg_in",
    )(x, g, w, lg, lb)


FF_CHUNK = 1024


def _ffn_tail(x1, g_ref, w1_ref, w2_ref, o_ref):
    xb = _rmsnorm(x1, g_ref[...]).astype(BF16)
    acc = x1
    for c in range(D_FF // FF_CHUNK):
        sl = slice(c * FF_CHUNK, (c + 1) * FF_CHUNK)
        h = jnp.maximum(_dot(xb, w1_ref[:, sl]), 0.0)
        acc = acc + _dot((h * h).astype(BF16), w2_ref[sl, :])
    o_ref[...] = acc


def _post_kernel(x_ref, a_ref, wo_ref, g_ref, w1_ref, w2_ref, o_ref):
    x1 = x_ref[...] + _dot(a_ref[...].astype(BF16), wo_ref[...])
    _ffn_tail(x1, g_ref, w1_ref, w2_ref, o_ref)


def _post_sg_prompt_kernel(x_ref, u_ref, v_ref, ws_ref, bs_ref, wo_ref, g_ref, w1_ref, w2_ref, o_ref,
                           a_ref):
    r = lax.broadcasted_iota(jnp.int32, (CHUNK, CHUNK), 0)
    c = lax.broadcasted_iota(jnp.int32, (CHUNK, CHUNK), 1)
    tril = r >= c
    for g in range(SG_GROUPS):
        w = jnp.where(tril, ws_ref[g], 0.0).astype(BF16)
        gl = slice(g * 128, (g + 1) * 128)
        for ch in range(x_ref.shape[0] // CHUNK):
            rs = slice(ch * CHUNK, (ch + 1) * CHUNK)
            s = _dot(w, v_ref[rs, gl].astype(BF16)) + bs_ref[:, gl]
            a_ref[rs, gl] = (u_ref[rs, gl] * s).astype(BF16)
    x1 = x_ref[...] + _dot(a_ref[...], wo_ref[...])
    _ffn_tail(x1, g_ref, w1_ref, w2_ref, o_ref)


def _post_sg_sample_kernel(x_ref, u_ref, v_ref, w0_ref, b0_ref, wo_ref, g_ref, w1_ref, w2_ref, o_ref):
    a = u_ref[...] * (v_ref[...] * w0_ref[...] + b0_ref[...])
    x1 = x_ref[...] + _dot(a.astype(BF16), wo_ref[...])
    _ffn_tail(x1, g_ref, w1_ref, w2_ref, o_ref)


def _post_call(kern, name, x, row_ins, full_ins, tm, scratch=()):
    m = x.shape[0]
    full = lambda a: pl.BlockSpec(a.shape, lambda i: (0,) * a.ndim, pipeline_mode=pl.Buffered(1))
    row = lambda a: pl.BlockSpec((tm, a.shape[1]), lambda i: (i, 0))
    return pl.pallas_call(
        kern,
        grid=(m // tm,),
        in_specs=[row(a) for a in row_ins] + [full(a) for a in full_ins],
        out_specs=pl.BlockSpec((tm, D_MODEL), lambda i: (i, 0)),
        out_shape=jax.ShapeDtypeStruct((m, D_MODEL), F32),
        scratch_shapes=list(scratch),
        compiler_params=_cparams(1),
        name=name,
    )(*row_ins, *full_ins)


def _final_kernel(x_ref, g_ref, o_ref):
    o_ref[...] = _rmsnorm(x_ref[...], g_ref[...])


def _final_norm(x, g, tm):
    m = x.shape[0]
    row = pl.BlockSpec((tm, D_MODEL), lambda i: (i, 0))
    return pl.pallas_call(
        _final_kernel,
        grid=(m // tm,),
        in_specs=[row, pl.BlockSpec(g.shape, lambda i: (0, 0))],
        out_specs=row,
        out_shape=jax.ShapeDtypeStruct((m, D_MODEL), F32),
        compiler_params=_cparams(1),
        name="final_norm",
    )(x, g)


def _expand_heads_cols(w):
    k = w.shape[0]
    w4 = w.reshape(k, N_HEADS, 1, HEAD_DIM)
    slot = (np.arange(N_HEADS)[:, None] // HEADS_PER_KV) == np.arange(N_KV)[None, :]
    out = jnp.where(jnp.asarray(slot)[None, :, :, None], w4, 0.0)
    return out.reshape(k, N_HEADS * N_KV * HEAD_DIM)


def kernel(x_prompt, x_sample, cache_nsa_kv, state_nsa_win, page_table, g_mix, g_ffn, g_final,
           nsa_w_in, nsa_b_gate, nsa_cmp_pos, nsa_cmp_w1, nsa_cmp_w2, nsa_w_out,
           sg_w_in, sg_ln_g, sg_ln_b, sg_w_spatial, sg_b_spatial, sg_w_out, ffn_w1, ffn_w2):
    nb, seq, _ = x_prompt.shape
    nd = x_sample.shape[0]
    depth = g_mix.shape[0]
    n_pool = cache_nsa_kv.shape[1]
    past = page_table.shape[1] * PAGE_SIZE
    tm_p = 512

    xp = x_prompt.reshape(nb * seq, D_MODEL)
    xs = x_sample.reshape(nd, D_MODEL)
    cache = cache_nsa_kv.transpose(0, 1, 3, 4, 5, 2).reshape(cache_nsa_kv.shape[0], n_pool, 4, 256,
                                                              PAGE_SIZE)
    win = state_nsa_win.transpose(0, 1, 3, 4, 5, 2).reshape(state_nsa_win.shape[0], nd, 2, 256, WINDOW)
    nwin = None

    tabs_p = _rope_tables(jnp.arange(seq, dtype=jnp.int32))
    tabs_s = _rope_tables(jnp.full((nd,), past, jnp.int32))
    tabs_c = _rope_tables(jnp.arange(LANES, dtype=jnp.int32) * CMP_STRIDE + CMP_BLOCK - 1)

    nidx = np.arange(LANES)
    ci = nidx[:, None] * CMP_STRIDE
    sj = nidx[None, :] * SLC_BLOCK
    ov_s = ((ci < sj + SLC_BLOCK) & (ci + CMP_BLOCK > sj) & (nidx[:, None] < LANES - 1))
    ov_s = jnp.asarray(ov_s.astype(np.float32), BF16)
    hsel = np.zeros((3, N_HEADS, LANES), np.float32)
    for c in range(3):
        hsel[c, np.arange(N_HEADS), c * N_HEADS + np.arange(N_HEADS)] = 1.0
    hsel = jnp.asarray(hsel)

    kv_p, win_p, kv_s, win_s, v_s = [], [], [], [], []
    for i in range(depth):
        j = i // 2
        gm = g_mix[i].reshape(1, D_MODEL)
        gf = g_ffn[i].reshape(1, D_MODEL)
        w1 = ffn_w1[i].astype(BF16)
        w2 = ffn_w2[i].astype(BF16)
        if i % 2 == 0:
            w_in = nsa_w_in[j]
            wq = w_in[:, 0:1024].astype(BF16)
            wq_exp = _expand_heads_cols(w_in[:, 0:1024]).astype(BF16)
            wkv = w_in[:, 1024:2048].astype(BF16)
            wkw = w_in[:, 2048:2560].astype(BF16)
            wg = jnp.pad(w_in[:, 2560:2608], ((0, 0), (0, LANES - 48))).astype(BF16)
            bgate = jnp.pad(nsa_b_gate[j], (0, LANES - 48)).reshape(1, LANES)
            w1c = nsa_cmp_w1[j]
            wcat = jnp.concatenate([w1c[:, :16].reshape(2, 1024, CMP_HID),
                                    w1c[:, 16:].reshape(2, 1024, CMP_HID)], axis=2).astype(BF16)
            pe = nsa_cmp_pos[j]
            pflat = jnp.concatenate([pe[:, :16].reshape(2, 1, 1024), pe[:, 16:].reshape(2, 1, 1024),
                                     jnp.zeros((2, 6, 1024), F32)], axis=1).astype(BF16)
            w2c = nsa_cmp_w2[j]
            zc = jnp.zeros_like(w2c)
            w2bd = jnp.concatenate([jnp.concatenate([w2c, zc], axis=2),
                                    jnp.concatenate([zc, w2c], axis=2)], axis=1).astype(BF16)
            wo = nsa_w_out[j].astype(BF16)
            wo_exp = _expand_heads_cols(nsa_w_out[j].T).T.astype(BF16)

            q_p, kv4_p, kvw_p, gt_p = _proj(xp, gm, tabs_p, seq // tm_p, wq, wkv, wkw, wg, tm_p)
            q_s, kv4_s, kvw_s, gt_s = _proj(xs, gm, tabs_s, 1, wq_exp, wkv, wkw, wg, nd)

            kc, vc = _cmp_prompt(kv4_p, nb, seq, wcat, pflat, w2bd, tabs_c)
            a_p = _attn_prompt(q_p, kv4_p, kvw_p, kc, vc, gt_p, bgate, nb, seq)

            q_exp = q_s.reshape(nd, N_HEADS, 256)
            oc_s, sel = _s1(j, page_table, cache, q_exp, wcat, pflat, w2bd, tabs_c, ov_s)
            sel_flat = sel[:, :N_KV, :8].reshape(-1)
            a_s, nwin = _s2(j, page_table, sel_flat, cache, win, q_exp, oc_s,
                            kv4_s.reshape(nd, 1, 1024), kvw_s.reshape(nd, 1, 512),
                            gt_s.reshape(nd, 1, LANES), bgate, hsel, nwin)

            xp = _post_call(_post_kernel, "post_nsa", xp, [xp, a_p], [wo, gf, w1, w2], tm_p)
            xs = _post_call(_post_kernel, "post_nsa", xs, [xs, a_s.reshape(nd, 4096)],
                            [wo_exp, gf, w1, w2], nd)

            kv_p.append(kv4_p.reshape(nb, seq, 4, N_KV, HEAD_DIM))
            win_p.append(kvw_p.reshape(nb, seq, 2, N_KV, HEAD_DIM)[:, seq - WINDOW:])
            kv_s.append(kv4_s.reshape(nd, 1, 4, N_KV, HEAD_DIM))
        else:
            w_in = sg_w_in[j].astype(BF16)
            lg = sg_ln_g[j].reshape(1, 1024)
            lb = sg_ln_b[j].reshape(1, 1024)
            ws = sg_w_spatial[j]
            bs = sg_b_spatial[j]
            bs_exp = jnp.repeat(bs.T, CHUNK, axis=1)
            w0 = jnp.repeat(ws[:, 0, 0], CHUNK).reshape(1, 1024)
            b0 = bs_exp[0:1]
            wo = sg_w_out[j].astype(BF16)

            u_p, v_p = _sg_in(xp, gm, w_in, lg, lb, tm_p)
            u_s, vv_s = _sg_in(xs, gm, w_in, lg, lb, nd)
            xp = _post_call(_post_sg_prompt_kernel, "post_sg_prompt", xp, [xp, u_p, v_p],
                            [ws, bs_exp, wo, gf, w1, w2], tm_p,
                            scratch=[pltpu.VMEM((tm_p, 1024), BF16)])
            xs = _post_call(_post_sg_sample_kernel, "post_sg_sample", xs, [xs, u_s, vv_s],
                            [w0, b0, wo, gf, w1, w2], nd)
            v_s.append(vv_s.reshape(nd, 1, 1024))

    gfin = g_final.reshape(1, D_MODEL)
    y_prompt = _final_norm(xp, gfin, tm_p).reshape(nb, seq, D_MODEL)
    y_sample = _final_norm(xs, gfin, nd).reshape(nd, 1, D_MODEL)
    new_win_s = nwin.reshape(nwin.shape[0], nd, 2, N_KV, HEAD_DIM, WINDOW).transpose(0, 1, 5, 2, 3, 4)
    return (y_prompt, y_sample, jnp.stack(kv_p), jnp.stack(win_p), jnp.stack(kv_s),
            new_win_s, jnp.stack(v_s))
```

```python
import functools

import numpy as np
import jax
import jax.numpy as jnp
from jax import lax
from jax.experimental import pallas as pl
from jax.experimental.pallas import tpu as pltpu

F32 = jnp.float32
BF16 = jnp.bfloat16

D_MODEL = 1024
N_HEADS = 16
HEAD_DIM = 64
N_KV = 4
HEADS_PER_KV = 4
ROPE_DIM = 16
ROPE_THETA = 500000.0
CMP_BLOCK = 32
CMP_STRIDE = 16
CMP_HID = 128
SLC_BLOCK = 64
TOP_N = 8
WINDOW = 512
PAGE_SIZE = 128
CHUNK = 128
SG_GROUPS = 8
D_FF = 4096
EPS = 1e-6
NEG = -1e30
FORCE = 1e6
SCALE = HEAD_DIM ** -0.5
LOG2E = 1.4426950408889634

LANES = 128
VMEM_LIMIT = 56 * 1024 * 1024


def _cparams(n_axes):
    return pltpu.CompilerParams(dimension_semantics=("arbitrary",) * n_axes,
                                vmem_limit_bytes=VMEM_LIMIT)


def _dot(a, b):
    return jnp.dot(a, b, preferred_element_type=F32)


def _dot_nt(a, b):
    return lax.dot_general(a, b, (((1,), (1,)), ((), ())), preferred_element_type=F32)


def _split_dot(a, b):
    hi = a.astype(BF16)
    lo = (a - hi.astype(F32)).astype(BF16)
    return _dot(hi, b) + _dot(lo, b)


def _rmsnorm(x, g):
    return x * lax.rsqrt(jnp.mean(x * x, axis=-1, keepdims=True) + EPS) * g


def _rope_slab(x, c, s1, s2):
    return x * c + pltpu.roll(x, LANES - 8, 1) * s1 + pltpu.roll(x, 8, 1) * s2


def _rope_tables(pos):
    half = ROPE_DIM // 2
    inv = jnp.power(jnp.float32(ROPE_THETA), -jnp.arange(half, dtype=F32) / half)
    ang = pos.astype(F32)[:, None] * inv[None, :]
    cos, sin = jnp.cos(ang), jnp.sin(ang)
    n = pos.shape[0]
    z = lambda w: jnp.zeros((n, w), F32)
    c = jnp.concatenate([cos, cos, jnp.ones((n, HEAD_DIM - ROPE_DIM), F32)], axis=1)
    s1 = jnp.concatenate([-sin, z(HEAD_DIM - half)], axis=1)
    s2 = jnp.concatenate([z(half), sin, z(HEAD_DIM - ROPE_DIM)], axis=1)
    t2 = lambda a: jnp.concatenate([a, a], axis=1)
    return t2(c), t2(s1), t2(s2)


def _proj_kernel(x_ref, g_ref, c_ref, s1_ref, s2_ref, wq_ref, wkv_ref, wkw_ref, wg_ref,
                 q_ref, kv_ref, kw_ref, gt_ref):
    xb = _rmsnorm(x_ref[...], g_ref[...]).astype(BF16)
    c, s1, s2 = c_ref[...], s1_ref[...], s2_ref[...]
    q = _dot(xb, wq_ref[...])
    for j in range(q.shape[1] // LANES):
        sl = slice(j * LANES, (j + 1) * LANES)
        q_ref[:, sl] = _rope_slab(q[:, sl], c, s1, s2)
    kv = _dot(xb, wkv_ref[...])
    kv_ref[:, 0:512] = kv[:, 0:512]
    for j in (4, 5):
        sl = slice(j * LANES, (j + 1) * LANES)
        kv_ref[:, sl] = _rope_slab(kv[:, sl], c, s1, s2)
    kv_ref[:, 768:1024] = kv[:, 768:1024]
    kw = _dot(xb, wkw_ref[...])
    for j in (0, 1):
        sl = slice(j * LANES, (j + 1) * LANES)
        kw_ref[:, sl] = _rope_slab(kw[:, sl], c, s1, s2)
    kw_ref[:, 256:512] = kw[:, 256:512]
    gt_ref[...] = _dot(xb, wg_ref[...])


def _proj(x, g, tabs, tab_period_blocks, wq, wkv, wkw, wg, tm):
    m = x.shape[0]
    nq = wq.shape[1]
    full = lambda a: pl.BlockSpec(a.shape, lambda i: (0,) * a.ndim, pipeline_mode=pl.Buffered(1))
    tab_spec = pl.BlockSpec((tm, LANES), lambda i: (i % tab_period_blocks, 0))
    row = lambda w: pl.BlockSpec((tm, w), lambda i: (i, 0))
    return pl.pallas_call(
        _proj_kernel,
        grid=(m // tm,),
        in_specs=[row(D_MODEL), full(g), tab_spec, tab_spec, tab_spec,
                  full(wq), full(wkv), full(wkw), full(wg)],
        out_specs=[row(nq), row(1024), row(512), row(LANES)],
        out_shape=[jax.ShapeDtypeStruct((m, nq), F32), jax.ShapeDtypeStruct((m, 1024), F32),
                   jax.ShapeDtypeStruct((m, 512), F32), jax.ShapeDtypeStruct((m, LANES), F32)],
        compiler_params=_cparams(1),
        name="nsa_proj",
    )(x, g, *tabs, wq, wkv, wkw, wg)


def _compress_slab(load, lhs_ref, wcat_ref, p_ref, w2_ref, kv):
    lane = lax.broadcasted_iota(jnp.int32, (128, LANES), 1)
    low = lane < 64
    for v in range(2):
        col = kv * 2 + v
        for lp in range(8):
            xe = load(2 * lp, col)
            xo = load(2 * lp + 1, col)
            re = pltpu.roll(xe, 64, 1)
            ro = pltpu.roll(xo, 64, 1)
            dst = slice(lp * LANES, (lp + 1) * LANES)
            lhs_ref[(2 * v) * 128:(2 * v + 1) * 128, dst] = jnp.where(low, xe, ro).astype(BF16)
            lhs_ref[(2 * v + 1) * 128:(2 * v + 2) * 128, dst] = jnp.where(low, re, xo).astype(BF16)
    w = wcat_ref[kv]
    c = _dot(lhs_ref[...], w)
    pb = _dot(p_ref[kv], w)
    bias = pb[0:1, 0:128] + pb[1:2, 128:256]
    hids = []
    for g in range(N_KV):
        lo = c[g * 128:(g + 1) * 128, 0:128]
        hi = c[g * 128:(g + 1) * 128, 128:256]
        pre = lo + pltpu.roll(hi, 127, 0) + bias
        hids.append(jax.nn.gelu(pre).astype(BF16))
    outs = []
    for pr in range(2):
        hc = jnp.concatenate([hids[2 * pr], hids[2 * pr + 1]], axis=1)
        outs.append(_dot(hc, w2_ref[kv]))
    return outs


def _cmp_prompt_kernel(x0_ref, x1_ref, x2_ref, x3_ref, wcat_ref, p_ref, w2_ref, c_ref, s1_ref, s2_ref,
                       kc_ref, vc_ref, lhs_ref):
    xs = (x0_ref, x1_ref, x2_ref, x3_ref)

    def load(l, cb):
        return xs[cb][pl.ds(l, 128, stride=CMP_STRIDE), :]
    ko = _compress_slab(load, lhs_ref, wcat_ref, p_ref, w2_ref, 0)
    vo = _compress_slab(load, lhs_ref, wcat_ref, p_ref, w2_ref, 1)
    for pr in range(2):
        kc_ref[pr] = _rope_slab(ko[pr], c_ref[...], s1_ref[...], s2_ref[...])
        vc_ref[pr] = vo[pr]


def _cmp_prompt(kv4, n_batch, seq, wcat, pflat, w2bd, ctabs):
    full = lambda a: pl.BlockSpec(a.shape, lambda b: (0,) * a.ndim)
    out = pl.BlockSpec((None, 2, 128, LANES), lambda b: (b, 0, 0, 0))
    return pl.pallas_call(
        _cmp_prompt_kernel,
        grid=(n_batch,),
        in_specs=[pl.BlockSpec((seq, LANES), functools.partial(lambda cb, b: (b, cb), cb))
                  for cb in range(4)] + [full(wcat), full(pflat), full(w2bd),
                  full(ctabs[0]), full(ctabs[1]), full(ctabs[2])],
        out_specs=[out, out],
        out_shape=[jax.ShapeDtypeStruct((n_batch, 2, 128, LANES), F32)] * 2,
        scratch_shapes=[pltpu.VMEM((512, 1024), BF16)],
        compiler_params=_cparams(1),
        name="nsa_cmp_prompt",
    )(kv4, kv4, kv4, kv4, wcat, pflat, w2bd, *ctabs)


QT = 128
COLS = 2 * HEADS_PER_KV * QT
TK_SLC = 512
WIN_KEYS = WINDOW + QT


def _tile_t(ref, r0, n_tiles):
    return jnp.concatenate([ref[pl.ds(r0 + i * LANES, LANES), :].T for i in range(n_tiles)], axis=1)


def _attn_kernel(q_ref, ks_ref, vs_ref, kw_ref, vw_ref, kc_ref, vc_ref, gt_ref, bg_ref,
                 ovt_ref, ext_ref, o_ref, acc_ref, gs_ref):
    gp = pl.program_id(1)
    qt = pl.program_id(2)

    zeros64 = jnp.zeros((HEAD_DIM, QT), F32)
    blocks = []
    for m in range(4):
        t = q_ref[:, m * LANES:(m + 1) * LANES].T * (SCALE * LOG2E)
        for par in range(2):
            dims = t[par * HEAD_DIM:(par + 1) * HEAD_DIM]
            blocks.append(jnp.concatenate([dims, zeros64] if m < 2 else [zeros64, dims], axis=0))
    qT = jnp.concatenate(blocks, axis=1).astype(BF16)

    def tile8(x):
        return jnp.concatenate([x] * 8, axis=1)

    sub = lax.broadcasted_iota(jnp.int32, (LANES, QT), 0)
    tq = qt * QT + lax.broadcasted_iota(jnp.int32, (LANES, QT), 1)
    ok_c = (sub * CMP_STRIDE + (CMP_BLOCK - 1)) <= tq
    sc = _dot(kc_ref[...].astype(BF16), qT) + tile8(jnp.where(ok_c, 0.0, NEG))
    mc = jnp.max(sc, axis=0, keepdims=True)
    ec = jnp.exp2(sc - mc) * tile8(jnp.where(ok_c, 1.0, 0.0))
    lc = jnp.sum(ec, axis=0, keepdims=True)
    pc = ec / jnp.where(lc > 0.0, lc, 1.0)
    o_c = _dot(vc_ref[...].T.astype(BF16), pc.astype(BF16))

    n_blk = 32
    ji = lax.broadcasted_iota(jnp.int32, (n_blk, QT), 0)
    jf = ji.astype(F32)
    cur = jnp.right_shift(qt * QT + lax.broadcasted_iota(jnp.int32, (n_blk, QT), 1), 6)
    allowed = ji <= cur
    forced = (ji == 0) | (ji == cur) | (ji == cur - 1)
    sel_bf = []
    for gi in range(2):
        pg = pc[:, (gi * 4) * QT:(gi * 4 + 1) * QT]
        for z in range(1, HEADS_PER_KV):
            pg = pg + pc[:, (gi * 4 + z) * QT:(gi * 4 + z + 1) * QT]
        hi = pg.astype(BF16)
        lo = (pg - hi.astype(F32)).astype(BF16)
        imp = _dot(ovt_ref[...], hi) + _dot(ovt_ref[...], lo)
        score = jnp.where(forced, FORCE, jnp.where(allowed, imp[0:n_blk], NEG))
        sel = jnp.zeros((n_blk, QT), F32)
        for _ in range(TOP_N):
            mx = jnp.max(score, axis=0, keepdims=True)
            first = jnp.min(jnp.where(score == mx, jf, 1e9), axis=0, keepdims=True)
            hit = jf == first
            sel = jnp.where(hit & (mx > NEG / 2), 1.0, sel)
            score = jnp.where(hit, -3e38, score)
        sel_bf.append(jnp.concatenate([sel, jnp.zeros((LANES - n_blk, QT), F32)], axis=0).astype(BF16))

    def slc_body(kt, carry):
        m_prev, l_prev = carry
        k0 = pl.multiple_of(kt * TK_SLC, TK_SLC)
        kpos = k0 + lax.broadcasted_iota(jnp.int32, (TK_SLC, QT), 0)
        causal = kpos <= qt * QT + lax.broadcasted_iota(jnp.int32, (TK_SLC, QT), 1)
        ex = ext_ref[pl.ds(k0, TK_SLC), :]
        parts = []
        for gi in range(2):
            mk = _dot(ex, sel_bf[gi])
            parts += [jnp.where((mk > 0.5) & causal, 0.0, NEG)] * HEADS_PER_KV
        s = _dot(ks_ref[pl.ds(k0, TK_SLC), :].astype(BF16), qT) + jnp.concatenate(parts, axis=1)
        m_new = jnp.maximum(m_prev, jnp.max(s, axis=0, keepdims=True))
        alpha = jnp.exp2(m_prev - m_new)
        p = jnp.exp2(s - m_new)
        l_new = alpha * l_prev + jnp.sum(p, axis=0, keepdims=True)
        vt = _tile_t(vs_ref, k0, TK_SLC // LANES).astype(BF16)
        acc_ref[...] = alpha * acc_ref[...] + _dot(vt, p.astype(BF16))
        return m_new, l_new

    acc_ref[...] = jnp.zeros(acc_ref.shape, F32)
    n_slc_tiles = jnp.right_shift(qt, 2) + 1
    _, l_s = lax.fori_loop(0, n_slc_tiles, slc_body,
                           (jnp.full((1, COLS), NEG, F32), jnp.zeros((1, COLS), F32)))
    o_s = acc_ref[...] / l_s

    w0 = pl.multiple_of(jnp.maximum(qt - WINDOW // QT, 0) * QT, QT)
    d = (qt * QT + lax.broadcasted_iota(jnp.int32, (WIN_KEYS, QT), 1)
         - (w0 + lax.broadcasted_iota(jnp.int32, (WIN_KEYS, QT), 0)))
    sw = (_dot(kw_ref[pl.ds(w0, WIN_KEYS), :].astype(BF16), qT)
          + tile8(jnp.where((d >= 0) & (d < WINDOW), 0.0, NEG)))
    pw = jnp.exp2(sw - jnp.max(sw, axis=0, keepdims=True))
    vwt = _tile_t(vw_ref, w0, WIN_KEYS // LANES).astype(BF16)
    o_w = _dot(vwt, pw.astype(BF16)) / jnp.sum(pw, axis=0, keepdims=True)

    gs_ref[...] = jax.nn.sigmoid(gt_ref[...] + bg_ref[...]).T

    def gate_row(c):
        g8 = gs_ref[pl.ds(pl.multiple_of(c * N_HEADS + gp * 8, 8), 8), :]
        return jnp.concatenate([g8[hh:hh + 1, :] for hh in range(8)], axis=1)

    comb = gate_row(0) * o_c + gate_row(1) * o_s + gate_row(2) * o_w
    for m in range(4):
        r0 = (m // 2) * HEAD_DIM
        top = comb[r0:r0 + HEAD_DIM, (2 * m) * QT:(2 * m + 1) * QT]
        bot = comb[r0:r0 + HEAD_DIM, (2 * m + 1) * QT:(2 * m + 2) * QT]
        o_ref[:, m * LANES:(m + 1) * LANES] = jnp.concatenate([top, bot], axis=0).T


def _attn_consts(seq):
    n = np.arange(LANES)
    ci = n[:, None] * CMP_STRIDE
    sj = n[None, :] * SLC_BLOCK
    n_cmp = (seq - CMP_BLOCK) // CMP_STRIDE + 1
    n_slc = -(-seq // SLC_BLOCK)
    ov = ((ci < sj + SLC_BLOCK) & (ci + CMP_BLOCK > sj) & (n[:, None] < n_cmp) & (n[None, :] < n_slc))
    ex = (np.arange(seq)[None, :] // SLC_BLOCK) == n[:, None]
    return (jnp.asarray(ov.T.astype(np.float32), BF16), jnp.asarray(ex.T.astype(np.float32), BF16))


def _attn_prompt(q, kv4, kvw, kc, vc, gates, bgate, n_batch, seq):
    ovt, ext = _attn_consts(seq)
    nqt = seq // QT
    kspec = lambda col0: pl.BlockSpec((seq, LANES), lambda b, gp, t: (b, col0 + gp))
    cspec = pl.BlockSpec((None, None, 128, LANES), lambda b, gp, t: (b, gp, 0, 0))
    return pl.pallas_call(
        _attn_kernel,
        grid=(n_batch, 2, nqt),
        in_specs=[
            pl.BlockSpec((QT, 512), lambda b, gp, t: (b * nqt + t, gp)),
            kspec(4), kspec(6), kspec(0), kspec(2), cspec, cspec,
            pl.BlockSpec((QT, LANES), lambda b, gp, t: (b * nqt + t, 0)),
            pl.BlockSpec((1, LANES), lambda b, gp, t: (0, 0)),
            pl.BlockSpec(ovt.shape, lambda b, gp, t: (0, 0)),
            pl.BlockSpec(ext.shape, lambda b, gp, t: (0, 0)),
        ],
        out_specs=pl.BlockSpec((QT, 512), lambda b, gp, t: (b * nqt + t, gp)),
        out_shape=jax.ShapeDtypeStruct((n_batch * seq, 1024), F32),
        scratch_shapes=[pltpu.VMEM((LANES, COLS), F32), pltpu.VMEM((LANES, QT), F32)],
        compiler_params=_cparams(3),
        name="nsa_attn_prompt",
    )(q, kv4, kv4, kvw, kvw, kc, vc, gates, bgate, ovt, ext)


N_PAGES = 16
SLAB_PITCH = 24


def _s1_kernel(layer, pt_ref, cache_ref, q_ref, wcat_ref, p_ref, w2_ref, c_ref, s1_ref, s2_ref,
               ov_ref, oc_ref, sel_ref, raw_ref, slab_ref, lhs_ref, sem_ref):
    b = pl.program_id(0)
    nb = pl.num_programs(0)

    def page_copy(bb, slot, p):
        return pltpu.make_async_copy(
            cache_ref.at[layer, pt_ref[bb, p], pl.ds(0, 2)],
            raw_ref.at[slot, p],
            sem_ref.at[slot])

    def start_all(bb, slot):
        for p in range(N_PAGES):
            page_copy(bb, slot, p).start()

    slot = b % 2

    @pl.when(b == 0)
    def _():
        start_all(0, 0)

    @pl.when(b + 1 < nb)
    def _():
        start_all(b + 1, 1 - slot)

    for p in range(N_PAGES):
        page_copy(b, slot, p).wait()

    chunks = PAGE_SIZE // CMP_STRIDE
    for p in range(N_PAGES):
        for cb in range(4):
            kind, pr = cb // 2, cb % 2
            t = raw_ref[slot, p, kind, pr * LANES:(pr + 1) * LANES, :].T
            for m in range(chunks):
                r0 = (p * chunks + m) * SLAB_PITCH
                slab_ref[cb, r0:r0 + CMP_STRIDE, :] = t[m * CMP_STRIDE:(m + 1) * CMP_STRIDE]

    def load(l, cb):
        return slab_ref[cb, pl.ds(l, 128, stride=SLAB_PITCH), :]

    ko = _compress_slab(load, lhs_ref, wcat_ref, p_ref, w2_ref, 0)
    vo = _compress_slab(load, lhs_ref, wcat_ref, p_ref, w2_ref, 1)
    kc = jnp.concatenate([_rope_slab(k, c_ref[...], s1_ref[...], s2_ref[...]) for k in ko], axis=1)
    vc = jnp.concatenate(vo, axis=1)

    qp = (q_ref[0] * SCALE).astype(BF16)
    sc = _dot_nt(qp, kc.astype(BF16))
    lane = lax.broadcasted_iota(jnp.int32, (N_HEADS, LANES), 1)
    ok = lane < (LANES - 1)
    sc = jnp.where(ok, sc, NEG)
    mc = jnp.max(sc, axis=1, keepdims=True)
    ec = jnp.where(ok, jnp.exp(sc - mc), 0.0)
    pc = ec / jnp.sum(ec, axis=1, keepdims=True)
    oc_ref[0] = _dot(pc.astype(BF16), vc.astype(BF16))

    imp_h = _split_dot(pc, ov_ref[...])
    rowg = jnp.right_shift(lax.broadcasted_iota(jnp.int32, (N_HEADS, LANES), 0), 2)
    row8 = lax.broadcasted_iota(jnp.int32, (8, LANES), 0)
    imp = jnp.zeros((8, LANES), F32)
    for g in range(N_KV):
        ig = jnp.sum(jnp.where(rowg == g, imp_h, 0.0), axis=0, keepdims=True)
        imp = jnp.where(row8 == g, ig, imp)
    lane8 = lax.broadcasted_iota(jnp.int32, (8, LANES), 1)
    jf = lane8.astype(F32)
    last = 2048 // SLC_BLOCK
    score = jnp.where((lane8 >= 1) & (lane8 <= last - 2), imp, NEG)
    picks = jnp.where(lane8 == 6, float(last - 1), 0.0)
    for k in range(TOP_N - 3):
        mx = jnp.max(score, axis=1, keepdims=True)
        first = jnp.min(jnp.where(score == mx, jf, 1e9), axis=1, keepdims=True)
        picks = jnp.where(lane8 == k, first, picks)
        score = jnp.where(jf == first, -3e38, score)
    sel_ref[0] = picks.astype(jnp.int32)


def _s1(layer, page_table, cache, q_exp, wcat, pflat, w2bd, ctabs, ov):
    nb = q_exp.shape[0]
    full = lambda a: pl.BlockSpec(a.shape, lambda b, pt: (0,) * a.ndim)
    grid_spec = pltpu.PrefetchScalarGridSpec(
        num_scalar_prefetch=1,
        grid=(nb,),
        in_specs=[pl.BlockSpec(memory_space=pl.ANY),
                  pl.BlockSpec((1, N_HEADS, 256), lambda b, pt: (b, 0, 0)),
                  full(wcat), full(pflat), full(w2bd), full(ctabs[0]), full(ctabs[1]), full(ctabs[2]),
                  full(ov)],
        out_specs=[pl.BlockSpec((1, N_HEADS, 256), lambda b, pt: (b, 0, 0)),
                   pl.BlockSpec((1, 8, LANES), lambda b, pt: (b, 0, 0))],
        scratch_shapes=[pltpu.VMEM((2, N_PAGES, 2, 256, PAGE_SIZE), F32),
                        pltpu.VMEM((4, 128 * SLAB_PITCH, LANES), F32),
                        pltpu.VMEM((512, 1024), BF16),
                        pltpu.SemaphoreType.DMA((2,))],
    )
    return pl.pallas_call(
        functools.partial(_s1_kernel, layer),
        grid_spec=grid_spec,
        out_shape=[jax.ShapeDtypeStruct((nb, N_HEADS, 256), F32),
                   jax.ShapeDtypeStruct((nb, 8, LANES), jnp.int32)],
        compiler_params=_cparams(1),
        name="nsa_sample_s1",
    )(page_table, cache, q_exp, wcat, pflat, w2bd, *ctabs, ov)


N_HIST = TOP_N - 1
KSEL = N_HIST * PAGE_SIZE


def _s2_kernel(layer, has_prev, pt_ref, sel_ref, cache_ref, win_ref, q_ref, oc_ref, kvn_ref, kwn_ref,
               gt_ref, bg_ref, hsel_ref, *rest):
    o_ref, nwin_ref, kbuf_ref, vbuf_ref, sem_ref = rest[1:] if has_prev else rest
    if has_prev:
        nwin_ref[0:layer] = rest[0][...]
    b = pl.program_id(0)
    nb = pl.num_programs(0)

    def copies(bb, slot, g, k):
        blk = sel_ref[(bb * N_KV + g) * 8 + k]
        page = pt_ref[bb, jnp.right_shift(blk, 1)]
        rows = pl.ds(g * HEAD_DIM, HEAD_DIM)
        dst = pl.ds(k * PAGE_SIZE, PAGE_SIZE)
        return (pltpu.make_async_copy(cache_ref.at[layer, page, 2, rows, :],
                                      kbuf_ref.at[slot, g, :, dst], sem_ref.at[slot]),
                pltpu.make_async_copy(cache_ref.at[layer, page, 3, rows, :],
                                      vbuf_ref.at[slot, g, :, dst], sem_ref.at[slot]))

    def start_all(bb, slot):
        for g in range(N_KV):
            for k in range(N_HIST):
                ck, cv = copies(bb, slot, g, k)
                ck.start()
                cv.start()

    slot = b % 2

    @pl.when(b == 0)
    def _():
        start_all(0, 0)

    @pl.when(b + 1 < nb)
    def _():
        start_all(b + 1, 1 - slot)

    for g in range(N_KV):
        for k in range(N_HIST):
            ck, cv = copies(b, slot, g, k)
            ck.wait()
            cv.wait()

    qf = q_ref[0] * SCALE
    qp = qf.astype(BF16)
    q16 = (qf[:, 0:64] + qf[:, 64:128] + qf[:, 128:192] + qf[:, 192:256]).astype(BF16)
    lane_k = lax.broadcasted_iota(jnp.int32, (N_HEADS, KSEL), 1)
    rowg_k = jnp.right_shift(lax.broadcasted_iota(jnp.int32, (N_HEADS, KSEL), 0), 2)
    rowg_d = jnp.right_shift(lax.broadcasted_iota(jnp.int32, (N_HEADS, HEAD_DIM), 0), 2)
    tile_k = jnp.right_shift(lane_k, 7)
    half_k = jnp.bitwise_and(jnp.right_shift(lane_k, 6), 1)

    def attend(s_hist, s_new, pv_hist, v_new):
        m = jnp.maximum(jnp.max(s_hist, axis=1, keepdims=True), s_new)
        e = jnp.exp(s_hist - m)
        en = jnp.exp(s_new - m)
        l = jnp.sum(e, axis=1, keepdims=True) + en
        return (pv_hist(e.astype(BF16)) + en * v_new) / l

    kn = kvn_ref[0]
    s_hist = jnp.zeros((N_HEADS, KSEL), F32)
    v16 = jnp.zeros((N_HEADS, HEAD_DIM), F32)
    for g in range(N_KV):
        sg = _dot(q16, kbuf_ref[slot, g].astype(BF16))
        want = jnp.zeros((N_HEADS, KSEL), jnp.int32)
        for k in range(N_HIST):
            half = jnp.bitwise_and(sel_ref[(b * N_KV + g) * 8 + k], 1)
            want = jnp.where(tile_k == k, half, want)
        s_hist = jnp.where(rowg_k == g, jnp.where(half_k == want, sg, NEG), s_hist)
        v16 = jnp.where(rowg_d == g, kn[:, 768 + g * HEAD_DIM:768 + (g + 1) * HEAD_DIM], v16)
    s_new = jnp.sum(qf * kn[:, 512:768], axis=1, keepdims=True)

    def pv_slc(e):
        o = jnp.zeros((N_HEADS, HEAD_DIM), F32)
        for g in range(N_KV):
            o = jnp.where(rowg_d == g, _dot_nt(e, vbuf_ref[slot, g].astype(BF16)), o)
        return o

    o_s = attend(s_hist, s_new, pv_slc, v16)
    o_s = jnp.concatenate([o_s] * N_KV, axis=1)

    wn = kwn_ref[0]
    s_w = _dot(qp, win_ref[0].astype(BF16))
    colw = lax.broadcasted_iota(jnp.int32, (N_HEADS, WINDOW), 1)
    s_w = jnp.where(colw >= 1, s_w, NEG)
    s_wn = jnp.sum(qf * wn[:, 0:256], axis=1, keepdims=True)
    vw = win_ref[1].astype(BF16)
    o_w = attend(s_w, s_wn, lambda e: _dot_nt(e, vw), wn[:, 256:512])

    lane = lax.broadcasted_iota(jnp.int32, (LANES, LANES), 1)
    for c in range(2):
        for rs in range(2):
            rows = slice(rs * LANES, (rs + 1) * LANES)
            new = wn[:, c * 256 + rs * LANES:c * 256 + (rs + 1) * LANES]
            col = jnp.broadcast_to(new, (LANES, LANES)).T
            tiles = [pltpu.roll(win_ref[c, rows, j * LANES:(j + 1) * LANES], LANES - 1, 1)
                     for j in range(WINDOW // LANES)]
            tiles.append(col)
            for j in range(WINDOW // LANES):
                nwin_ref[layer, c, rows, j * LANES:(j + 1) * LANES] = jnp.where(
                    lane == LANES - 1, tiles[j + 1], tiles[j])

    gsig = jax.nn.sigmoid(gt_ref[0] + bg_ref[...])
    gcol = lambda c: jnp.sum(hsel_ref[c] * gsig, axis=1, keepdims=True)
    o_ref[0] = gcol(0) * oc_ref[0] + gcol(1) * o_s + gcol(2) * o_w


def _s2(layer, page_table, sel_flat, cache, win, q_exp, oc, kv4s, kvws, gates, bgate, hsel, prev):
    nb = q_exp.shape[0]
    hq = pl.BlockSpec((1, N_HEADS, 256), lambda b, pt, sl: (b, 0, 0))
    row = lambda w: pl.BlockSpec((1, 1, w), lambda b, pt, sl: (b, 0, 0))
    wspec = pl.BlockSpec((None, None, 2, 256, WINDOW), lambda b, pt, sl: (layer, b, 0, 0, 0))
    stack = lambda n: pl.BlockSpec((n, None, 2, 256, WINDOW), lambda b, pt, sl: (0, b, 0, 0, 0))
    in_specs = [pl.BlockSpec(memory_space=pl.ANY), wspec,
                hq, hq, row(1024), row(512), row(LANES),
                pl.BlockSpec((1, LANES), lambda b, pt, sl: (0, 0)),
                pl.BlockSpec(hsel.shape, lambda b, pt, sl: (0, 0, 0))]
    args = [page_table, sel_flat, cache, win, q_exp, oc, kv4s, kvws, gates, bgate, hsel]
    if prev is not None:
        in_specs.append(stack(layer))
        args.append(prev)
    grid_spec = pltpu.PrefetchScalarGridSpec(
        num_scalar_prefetch=2,
        grid=(nb,),
        in_specs=in_specs,
        out_specs=[hq, stack(layer + 1)],
        scratch_shapes=[pltpu.VMEM((2, N_KV, HEAD_DIM, KSEL), F32),
                        pltpu.VMEM((2, N_KV, HEAD_DIM, KSEL), F32),
                        pltpu.SemaphoreType.DMA((2,))],
    )
    return pl.pallas_call(
        functools.partial(_s2_kernel, layer, prev is not None),
        grid_spec=grid_spec,
        out_shape=[jax.ShapeDtypeStruct((nb, N_HEADS, 256), F32),
                   jax.ShapeDtypeStruct((layer + 1, nb, 2, 256, WINDOW), F32)],
        compiler_params=_cparams(1),
        name="nsa_sample_s2",
    )(*args)


def _sg_in_kernel(x_ref, g_ref, w_ref, lg_ref, lb_ref, u_ref, v_ref):
    xb = _rmsnorm(x_ref[...], g_ref[...]).astype(BF16)
    u_ref[...] = jax.nn.gelu(_dot(xb, w_ref[:, 0:1024]))
    v = jax.nn.gelu(_dot(xb, w_ref[:, 1024:2048]))
    mu = jnp.mean(v, axis=-1, keepdims=True)
    var = jnp.mean(jnp.square(v - mu), axis=-1, keepdims=True)
    v_ref[...] = (v - mu) * lax.rsqrt(var + EPS) * lg_ref[...] + lb_ref[...]


def _sg_in(x, g, w, lg, lb, tm):
    m = x.shape[0]
    full = lambda a: pl.BlockSpec(a.shape, lambda i: (0,) * a.ndim, pipeline_mode=pl.Buffered(1))
    row = pl.BlockSpec((tm, 1024), lambda i: (i, 0))
    return pl.pallas_call(
        _sg_in_kernel,
        grid=(m // tm,),
        in_specs=[row, full(g), full(w), full(lg), full(lb)],
        out_specs=[row, row],
        out_shape=[jax.ShapeDtypeStruct((m, 1024), F32)] * 2,
        compiler_params=_cparams(1),
        name="sg_in",
    )(x, g, w, lg, lb)


FF_CHUNK = 1024


def _ffn_tail(x1, g_ref, w1_ref, w2_ref, o_ref):
    xb = _rmsnorm(x1, g_ref[...]).astype(BF16)
    acc = x1
    for c in range(D_FF // FF_CHUNK):
        sl = slice(c * FF_CHUNK, (c + 1) * FF_CHUNK)
        h = jnp.maximum(_dot(xb, w1_ref[:, sl]), 0.0)
        acc = acc + _dot((h * h).astype(BF16), w2_ref[sl, :])
    o_ref[...] = acc


def _post_kernel(x_ref, a_ref, wo_ref, g_ref, w1_ref, w2_ref, o_ref):
    x1 = x_ref[...] + _dot(a_ref[...].astype(BF16), wo_ref[...])
    _ffn_tail(x1, g_ref, w1_ref, w2_ref, o_ref)


def _post_sg_prompt_kernel(x_ref, u_ref, v_ref, ws_ref, bs_ref, wo_ref, g_ref, w1_ref, w2_ref, o_ref,
                           a_ref):
    r = lax.broadcasted_iota(jnp.int32, (CHUNK, CHUNK), 0)
    c = lax.broadcasted_iota(jnp.int32, (CHUNK, CHUNK), 1)
    tril = r >= c
    for g in range(SG_GROUPS):
        w = jnp.where(tril, ws_ref[g], 0.0).astype(BF16)
        gl = slice(g * 128, (g + 1) * 128)
        for ch in range(x_ref.shape[0] // CHUNK):
            rs = slice(ch * CHUNK, (ch + 1) * CHUNK)
            s = _dot(w, v_ref[rs, gl].astype(BF16)) + bs_ref[:, gl]
            a_ref[rs, gl] = (u_ref[rs, gl] * s).astype(BF16)
    x1 = x_ref[...] + _dot(a_ref[...], wo_ref[...])
    _ffn_tail(x1, g_ref, w1_ref, w2_ref, o_ref)


def _post_sg_sample_kernel(x_ref, u_ref, v_ref, w0_ref, b0_ref, wo_ref, g_ref, w1_ref, w2_ref, o_ref):
    a = u_ref[...] * (v_ref[...] * w0_ref[...] + b0_ref[...])
    x1 = x_ref[...] + _dot(a.astype(BF16), wo_ref[...])
    _ffn_tail(x1, g_ref, w1_ref, w2_ref, o_ref)


def _post_call(kern, name, x, row_ins, full_ins, tm, scratch=()):
    m = x.shape[0]
    full = lambda a: pl.BlockSpec(a.shape, lambda i: (0,) * a.ndim, pipeline_mode=pl.Buffered(1))
    row = lambda a: pl.BlockSpec((tm, a.shape[1]), lambda i: (i, 0))
    return pl.pallas_call(
        kern,
        grid=(m // tm,),
        in_specs=[row(a) for a in row_ins] + [full(a) for a in full_ins],
        out_specs=pl.BlockSpec((tm, D_MODEL), lambda i: (i, 0)),
        out_shape=jax.ShapeDtypeStruct((m, D_MODEL), F32),
        scratch_shapes=list(scratch),
        compiler_params=_cparams(1),
        name=name,
    )(*row_ins, *full_ins)


def _final_kernel(x_ref, g_ref, o_ref):
    o_ref[...] = _rmsnorm(x_ref[...], g_ref[...])


def _final_norm(x, g, tm):
    m = x.shape[0]
    row = pl.BlockSpec((tm, D_MODEL), lambda i: (i, 0))
    return pl.pallas_call(
        _final_kernel,
        grid=(m // tm,),
        in_specs=[row, pl.BlockSpec(g.shape, lambda i: (0, 0))],
        out_specs=row,
        out_shape=jax.ShapeDtypeStruct((m, D_MODEL), F32),
        compiler_params=_cparams(1),
        name="final_norm",
    )(x, g)


def _expand_heads_cols(w):
    k = w.shape[0]
    w4 = w.reshape(k, N_HEADS, 1, HEAD_DIM)
    slot = (np.arange(N_HEADS)[:, None] // HEADS_PER_KV) == np.arange(N_KV)[None, :]
    out = jnp.where(jnp.asarray(slot)[None, :, :, None], w4, 0.0)
    return out.reshape(k, N_HEADS * N_KV * HEAD_DIM)


def kernel(x_prompt, x_sample, cache_nsa_kv, state_nsa_win, page_table, g_mix, g_ffn, g_final,
           nsa_w_in, nsa_b_gate, nsa_cmp_pos, nsa_cmp_w1, nsa_cmp_w2, nsa_w_out,
           sg_w_in, sg_ln_g, sg_ln_b, sg_w_spatial, sg_b_spatial, sg_w_out, ffn_w1, ffn_w2):
    nb, seq, _ = x_prompt.shape
    nd = x_sample.shape[0]
    depth = g_mix.shape[0]
    n_pool = cache_nsa_kv.shape[1]
    past = page_table.shape[1] * PAGE_SIZE
    tm_p = 512

    xp = x_prompt.reshape(nb * seq, D_MODEL)
    xs = x_sample.reshape(nd, D_MODEL)
    cache = cache_nsa_kv.transpose(0, 1, 3, 4, 5, 2).reshape(cache_nsa_kv.shape[0], n_pool, 4, 256,
                                                              PAGE_SIZE)
    win = state_nsa_win.transpose(0, 1, 3, 4, 5, 2).reshape(state_nsa_win.shape[0], nd, 2, 256, WINDOW)
    nwin = None

    tabs_p = _rope_tables(jnp.arange(seq, dtype=jnp.int32))
    tabs_s = _rope_tables(jnp.full((nd,), past, jnp.int32))
    tabs_c = _rope_tables(jnp.arange(LANES, dtype=jnp.int32) * CMP_STRIDE + CMP_BLOCK - 1)

    nidx = np.arange(LANES)
    ci = nidx[:, None] * CMP_STRIDE
    sj = nidx[None, :] * SLC_BLOCK
    ov_s = ((ci < sj + SLC_BLOCK) & (ci + CMP_BLOCK > sj) & (nidx[:, None] < LANES - 1))
    ov_s = jnp.asarray(ov_s.astype(np.float32), BF16)
    hsel = np.zeros((3, N_HEADS, LANES), np.float32)
    for c in range(3):
        hsel[c, np.arange(N_HEADS), c * N_HEADS + np.arange(N_HEADS)] = 1.0
    hsel = jnp.asarray(hsel)

    kv_p, win_p, kv_s, win_s, v_s = [], [], [], [], []
    for i in range(depth):
        j = i // 2
        gm = g_mix[i].reshape(1, D_MODEL)
        gf = g_ffn[i].reshape(1, D_MODEL)
        w1 = ffn_w1[i].astype(BF16)
        w2 = ffn_w2[i].astype(BF16)
        if i % 2 == 0:
            w_in = nsa_w_in[j]
            wq = w_in[:, 0:1024].astype(BF16)
            wq_exp = _expand_heads_cols(w_in[:, 0:1024]).astype(BF16)
            wkv = w_in[:, 1024:2048].astype(BF16)
            wkw = w_in[:, 2048:2560].astype(BF16)
            wg = jnp.pad(w_in[:, 2560:2608], ((0, 0), (0, LANES - 48))).astype(BF16)
            bgate = jnp.pad(nsa_b_gate[j], (0, LANES - 48)).reshape(1, LANES)
            w1c = nsa_cmp_w1[j]
            wcat = jnp.concatenate([w1c[:, :16].reshape(2, 1024, CMP_HID),
                                    w1c[:, 16:].reshape(2, 1024, CMP_HID)], axis=2).astype(BF16)
            pe = nsa_cmp_pos[j]
            pflat = jnp.concatenate([pe[:, :16].reshape(2, 1, 1024), pe[:, 16:].reshape(2, 1, 1024),
                                     jnp.zeros((2, 6, 1024), F32)], axis=1).astype(BF16)
            w2c = nsa_cmp_w2[j]
            zc = jnp.zeros_like(w2c)
            w2bd = jnp.concatenate([jnp.concatenate([w2c, zc], axis=2),
                                    jnp.concatenate([zc, w2c], axis=2)], axis=1).astype(BF16)
            wo = nsa_w_out[j].astype(BF16)
            wo_exp = _expand_heads_cols(nsa_w_out[j].T).T.astype(BF16)

            q_p, kv4_p, kvw_p, gt_p = _proj(xp, gm, tabs_p, seq // tm_p, wq, wkv, wkw, wg, tm_p)
            q_s, kv4_s, kvw_s, gt_s = _proj(xs, gm, tabs_s, 1, wq_exp, wkv, wkw, wg, nd)

            kc, vc = _cmp_prompt(kv4_p, nb, seq, wcat, pflat, w2bd, tabs_c)
            a_p = _attn_prompt(q_p, kv4_p, kvw_p, kc, vc, gt_p, bgate, nb, seq)

            q_exp = q_s.reshape(nd, N_HEADS, 256)
            oc_s, sel = _s1(j, page_table, cache, q_exp, wcat, pflat, w2bd, tabs_c, ov_s)
            sel_flat = sel[:, :N_KV, :8].reshape(-1)
            a_s, nwin = _s2(j, page_table, sel_flat, cache, win, q_exp, oc_s,
                            kv4_s.reshape(nd, 1, 1024), kvw_s.reshape(nd, 1, 512),
                            gt_s.reshape(nd, 1, LANES), bgate, hsel, nwin)

            xp = _post_call(_post_kernel, "post_nsa", xp, [xp, a_p], [wo, gf, w1, w2], tm_p)
            xs = _post_call(_post_kernel, "post_nsa", xs, [xs, a_s.reshape(nd, 4096)],
                            [wo_exp, gf, w1, w2], nd)

            kv_p.append(kv4_p.reshape(nb, seq, 4, N_KV, HEAD_DIM))
            win_p.append(kvw_p.reshape(nb, seq, 2, N_KV, HEAD_DIM)[:, seq - WINDOW:])
            kv_s.append(kv4_s.reshape(nd, 1, 4, N_KV, HEAD_DIM))
        else:
            w_in = sg_w_in[j].astype(BF16)
            lg = sg_ln_g[j].reshape(1, 1024)
            lb = sg_ln_b[j].reshape(1, 1024)
            ws = sg_w_spatial[j]
            bs = sg_b_spatial[j]
            bs_exp = jnp.repeat(bs.T, CHUNK, axis=1)
            w0 = jnp.repeat(ws[:, 0, 0], CHUNK).reshape(1, 1024)
            b0 = bs_exp[0:1]
            wo = sg_w_out[j].astype(BF16)

            u_p, v_p = _sg_in(xp, gm, w_in, lg, lb, tm_p)
            u_s, vv_s = _sg_in(xs, gm, w_in, lg, lb, nd)
            xp = _post_call(_post_sg_prompt_kernel, "post_sg_prompt", xp, [xp, u_p, v_p],
                            [ws, bs_exp, wo, gf, w1, w2], tm_p,
                            scratch=[pltpu.VMEM((tm_p, 1024), BF16)])
            xs = _post_call(_post_sg_sample_kernel, "post_sg_sample", xs, [xs, u_s, vv_s],
                            [w0, b0, wo, gf, w1, w2], nd)
            v_s.append(vv_s.reshape(nd, 1, 1024))

    gfin = g_final.reshape(1, D_MODEL)
    y_prompt = _final_norm(xp, gfin, tm_p).reshape(nb, seq, D_MODEL)
    y_sample = _final_norm(xs, gfin, nd).reshape(nd, 1, D_MODEL)
    new_win_s = nwin.reshape(nwin.shape[0], nd, 2, N_KV, HEAD_DIM, WINDOW).transpose(0, 1, 5, 2, 3, 4)
    return (y_prompt, y_sample, jnp.stack(kv_p), jnp.stack(win_p), jnp.stack(kv_s),
            new_win_s, jnp.stack(v_s))
```

```python
import functools

import numpy as np
import jax
import jax.numpy as jnp
from jax import lax
from jax.experimental import pallas as pl
from jax.experimental.pallas import tpu as pltpu

F32 = jnp.float32
BF16 = jnp.bfloat16

D_MODEL = 1024
N_HEADS = 16
HEAD_DIM = 64
N_KV = 4
HEADS_PER_KV = 4
ROPE_DIM = 16
ROPE_THETA = 500000.0
CMP_BLOCK = 32
CMP_STRIDE = 16
CMP_HID = 128
SLC_BLOCK = 64
TOP_N = 8
WINDOW = 512
PAGE_SIZE = 128
CHUNK = 128
SG_GROUPS = 8
D_FF = 4096
EPS = 1e-6
NEG = -1e30
FORCE = 1e6
SCALE = HEAD_DIM ** -0.5
LOG2E = 1.4426950408889634

LANES = 128
VMEM_LIMIT = 56 * 1024 * 1024


def _cparams(n_axes):
    return pltpu.CompilerParams(dimension_semantics=("arbitrary",) * n_axes,
                                vmem_limit_bytes=VMEM_LIMIT)


def _dot(a, b):
    return jnp.dot(a, b, preferred_element_type=F32)


def _dot_nt(a, b):
    return lax.dot_general(a, b, (((1,), (1,)), ((), ())), preferred_element_type=F32)


def _split_dot(a, b):
    hi = a.astype(BF16)
    lo = (a - hi.astype(F32)).astype(BF16)
    return _dot(hi, b) + _dot(lo, b)


def _rmsnorm(x, g):
    return x * lax.rsqrt(jnp.mean(x * x, axis=-1, keepdims=True) + EPS) * g


def _rope_slab(x, c, s1, s2):
    return x * c + pltpu.roll(x, LANES - 8, 1) * s1 + pltpu.roll(x, 8, 1) * s2


def _rope_tables(pos):
    half = ROPE_DIM // 2
    inv = jnp.power(jnp.float32(ROPE_THETA), -jnp.arange(half, dtype=F32) / half)
    ang = pos.astype(F32)[:, None] * inv[None, :]
    cos, sin = jnp.cos(ang), jnp.sin(ang)
    n = pos.shape[0]
    z = lambda w: jnp.zeros((n, w), F32)
    c = jnp.concatenate([cos, cos, jnp.ones((n, HEAD_DIM - ROPE_DIM), F32)], axis=1)
    s1 = jnp.concatenate([-sin, z(HEAD_DIM - half)], axis=1)
    s2 = jnp.concatenate([z(half), sin, z(HEAD_DIM - ROPE_DIM)], axis=1)
    t2 = lambda a: jnp.concatenate([a, a], axis=1)
    return t2(c), t2(s1), t2(s2)


def _proj_kernel(stack_t, layer, tiles_per_seq, x_ref, g_ref, c_ref, s1_ref, s2_ref, wq_ref, wkv_ref,
                 wkw_ref, wg_ref, *rest):
    if stack_t and layer > 0:
        pkv_ref, pkw_ref = rest[:2]
        rest = rest[2:]
    q_ref, kv_ref, kw_ref, gt_ref = rest[:4]
    xb = _rmsnorm(x_ref[...], g_ref[...]).astype(BF16)
    c, s1, s2 = c_ref[...], s1_ref[...], s2_ref[...]
    q = _dot(xb, wq_ref[...])
    for j in range(q.shape[1] // LANES):
        sl = slice(j * LANES, (j + 1) * LANES)
        q_ref[:, sl] = _rope_slab(q[:, sl], c, s1, s2)
    kv = _dot(xb, wkv_ref[...])
    kv_ref[:, 0:512] = kv[:, 0:512]
    for j in (4, 5):
        sl = slice(j * LANES, (j + 1) * LANES)
        kv_ref[:, sl] = _rope_slab(kv[:, sl], c, s1, s2)
    kv_ref[:, 768:1024] = kv[:, 768:1024]
    kw = _dot(xb, wkw_ref[...])
    for j in (0, 1):
        sl = slice(j * LANES, (j + 1) * LANES)
        kw_ref[:, sl] = _rope_slab(kw[:, sl], c, s1, s2)
    kw_ref[:, 256:512] = kw[:, 256:512]
    gt_ref[...] = _dot(xb, wg_ref[...])
    if not stack_t:
        return
    kvt_ref, kwt_ref = rest[4:6]
    tm = x_ref.shape[0]

    def put_t(dst_ref, src_ref, width):
        for r in range(tm // LANES):
            for cb in range(width // LANES):
                dst_ref[layer, cb * LANES:(cb + 1) * LANES, r * LANES:(r + 1) * LANES] = (
                    src_ref[r * LANES:(r + 1) * LANES, cb * LANES:(cb + 1) * LANES].T)

    if layer > 0:
        kvt_ref[0:layer] = pkv_ref[...]
    put_t(kvt_ref, kv_ref, 1024)

    @pl.when(pl.program_id(0) % tiles_per_seq == tiles_per_seq - 1)
    def _():
        if layer > 0:
            kwt_ref[0:layer] = pkw_ref[...]
        put_t(kwt_ref, kw_ref, 512)


def _proj(x, g, tabs, tab_period_blocks, wq, wkv, wkw, wg, tm, stack=None):
    m = x.shape[0]
    nq = wq.shape[1]
    full = lambda a: pl.BlockSpec(a.shape, lambda i: (0,) * a.ndim, pipeline_mode=pl.Buffered(1))
    tab_spec = pl.BlockSpec((tm, LANES), lambda i: (i % tab_period_blocks, 0))
    row = lambda w: pl.BlockSpec((tm, w), lambda i: (i, 0))
    in_specs = [row(D_MODEL), full(g), tab_spec, tab_spec, tab_spec,
                full(wq), full(wkv), full(wkw), full(wg)]
    args = [x, g, *tabs, wq, wkv, wkw, wg]
    out_specs = [row(nq), row(1024), row(512), row(LANES)]
    out_shape = [jax.ShapeDtypeStruct((m, nq), F32), jax.ShapeDtypeStruct((m, 1024), F32),
                 jax.ShapeDtypeStruct((m, 512), F32), jax.ShapeDtypeStruct((m, LANES), F32)]
    layer, tps = 0, 1
    if stack is not None:
        layer, n_batch, seq, prev_kvt, prev_kwt = stack
        assert tm == WINDOW and seq % tm == 0
        tps = seq // tm
        kvt = lambda n: pl.BlockSpec((n, None, 1024, tm), lambda i: (0, i // tps, 0, i % tps))
        kwt = lambda n: pl.BlockSpec((n, None, 512, WINDOW), lambda i: (0, i // tps, 0, 0))
        if layer > 0:
            in_specs += [kvt(layer), kwt(layer)]
            args += [prev_kvt, prev_kwt]
        out_specs += [kvt(layer + 1), kwt(layer + 1)]
        out_shape += [jax.ShapeDtypeStruct((layer + 1, n_batch, 1024, seq), F32),
                      jax.ShapeDtypeStruct((layer + 1, n_batch, 512, WINDOW), F32)]
    return pl.pallas_call(
        functools.partial(_proj_kernel, stack is not None, layer, tps),
        grid=(m // tm,),
        in_specs=in_specs,
        out_specs=out_specs,
        out_shape=out_shape,
        compiler_params=_cparams(1),
        name="nsa_proj",
    )(*args)


def _fill_lhs_strided(load, lhs_ref, kv):
    lane = lax.broadcasted_iota(jnp.int32, (128, LANES), 1)
    low = lane < 64
    for v in range(2):
        col = kv * 2 + v
        for lp in range(8):
            xe = load(2 * lp, col)
            xo = load(2 * lp + 1, col)
            re = pltpu.roll(xe, 64, 1)
            ro = pltpu.roll(xo, 64, 1)
            dst = slice(lp * LANES, (lp + 1) * LANES)
            lhs_ref[(2 * v) * 128:(2 * v + 1) * 128, dst] = jnp.where(low, xe, ro).astype(BF16)
            lhs_ref[(2 * v + 1) * 128:(2 * v + 2) * 128, dst] = jnp.where(low, re, xo).astype(BF16)


def _compress_mlp(lhs_ref, wcat_ref, p_ref, w2_ref, kv):
    w = wcat_ref[kv]
    c = _dot(lhs_ref[...], w)
    pb = _dot(p_ref[kv], w)
    bias = pb[0:1, 0:128] + pb[1:2, 128:256]
    hids = []
    for g in range(N_KV):
        lo = c[g * 128:(g + 1) * 128, 0:128]
        hi = c[g * 128:(g + 1) * 128, 128:256]
        pre = lo + pltpu.roll(hi, 127, 0) + bias
        hids.append(jax.nn.gelu(pre).astype(BF16))
    outs = []
    for pr in range(2):
        hc = jnp.concatenate([hids[2 * pr], hids[2 * pr + 1]], axis=1)
        outs.append(_dot(hc, w2_ref[kv]))
    return outs


def _cmp_prompt_kernel(x0_ref, x1_ref, x2_ref, x3_ref, wcat_ref, p_ref, w2_ref, c_ref, s1_ref, s2_ref,
                       kc_ref, vc_ref, lhs_ref):
    xs = (x0_ref, x1_ref, x2_ref, x3_ref)

    def load(l, cb):
        return xs[cb][pl.ds(l, 128, stride=CMP_STRIDE), :]
    outs = []
    for kv in range(2):
        _fill_lhs_strided(load, lhs_ref, kv)
        outs.append(_compress_mlp(lhs_ref, wcat_ref, p_ref, w2_ref, kv))
    ko, vo = outs
    for pr in range(2):
        kc_ref[pr] = _rope_slab(ko[pr], c_ref[...], s1_ref[...], s2_ref[...])
        vc_ref[pr] = vo[pr]


def _cmp_prompt(kv4, n_batch, seq, wcat, pflat, w2bd, ctabs):
    full = lambda a: pl.BlockSpec(a.shape, lambda b: (0,) * a.ndim)
    out = pl.BlockSpec((None, 2, 128, LANES), lambda b: (b, 0, 0, 0))
    return pl.pallas_call(
        _cmp_prompt_kernel,
        grid=(n_batch,),
        in_specs=[pl.BlockSpec((seq, LANES), functools.partial(lambda cb, b: (b, cb), cb))
                  for cb in range(4)] + [full(wcat), full(pflat), full(w2bd),
                  full(ctabs[0]), full(ctabs[1]), full(ctabs[2])],
        out_specs=[out, out],
        out_shape=[jax.ShapeDtypeStruct((n_batch, 2, 128, LANES), F32)] * 2,
        scratch_shapes=[pltpu.VMEM((512, 1024), BF16)],
        compiler_params=_cparams(1),
        name="nsa_cmp_prompt",
    )(kv4, kv4, kv4, kv4, wcat, pflat, w2bd, *ctabs)


QT = 128
COLS = 2 * HEADS_PER_KV * QT
TK_SLC = 512
WIN_KEYS = WINDOW + QT


def _tile_t(ref, r0, n_tiles):
    return jnp.concatenate([ref[pl.ds(r0 + i * LANES, LANES), :].T for i in range(n_tiles)], axis=1)


def _attn_kernel(q_ref, ks_ref, vs_ref, kw_ref, vw_ref, kc_ref, vc_ref, gt_ref, bg_ref,
                 ovt_ref, ext_ref, o_ref, acc_ref, gs_ref):
    gp = pl.program_id(1)
    qt = pl.program_id(2)

    zeros64 = jnp.zeros((HEAD_DIM, QT), F32)
    blocks = []
    for m in range(4):
        t = q_ref[:, m * LANES:(m + 1) * LANES].T * (SCALE * LOG2E)
        for par in range(2):
            dims = t[par * HEAD_DIM:(par + 1) * HEAD_DIM]
            blocks.append(jnp.concatenate([dims, zeros64] if m < 2 else [zeros64, dims], axis=0))
    qT = jnp.concatenate(blocks, axis=1).astype(BF16)

    def tile8(x):
        return jnp.concatenate([x] * 8, axis=1)

    sub = lax.broadcasted_iota(jnp.int32, (LANES, QT), 0)
    tq = qt * QT + lax.broadcasted_iota(jnp.int32, (LANES, QT), 1)
    ok_c = (sub * CMP_STRIDE + (CMP_BLOCK - 1)) <= tq
    sc = _dot(kc_ref[...].astype(BF16), qT) + tile8(jnp.where(ok_c, 0.0, NEG))
    mc = jnp.max(sc, axis=0, keepdims=True)
    ec = jnp.exp2(sc - mc) * tile8(jnp.where(ok_c, 1.0, 0.0))
    lc = jnp.sum(ec, axis=0, keepdims=True)
    pc = ec / jnp.where(lc > 0.0, lc, 1.0)
    o_c = _dot(vc_ref[...].T.astype(BF16), pc.astype(BF16))

    n_blk = 32
    ji = lax.broadcasted_iota(jnp.int32, (n_blk, QT), 0)
    jf = ji.astype(F32)
    cur = jnp.right_shift(qt * QT + lax.broadcasted_iota(jnp.int32, (n_blk, QT), 1), 6)
    allowed = ji <= cur
    forced = (ji == 0) | (ji == cur) | (ji == cur - 1)
    sel_bf = []
    for gi in range(2):
        pg = pc[:, (gi * 4) * QT:(gi * 4 + 1) * QT]
        for z in range(1, HEADS_PER_KV):
            pg = pg + pc[:, (gi * 4 + z) * QT:(gi * 4 + z + 1) * QT]
        hi = pg.astype(BF16)
        lo = (pg - hi.astype(F32)).astype(BF16)
        imp = _dot(ovt_ref[...], hi) + _dot(ovt_ref[...], lo)
        score = jnp.where(forced, FORCE, jnp.where(allowed, imp[0:n_blk], NEG))
        sel = jnp.zeros((n_blk, QT), F32)
        for _ in range(TOP_N):
            mx = jnp.max(score, axis=0, keepdims=True)
            first = jnp.min(jnp.where(score == mx, jf, 1e9), axis=0, keepdims=True)
            hit = jf == first
            sel = jnp.where(hit & (mx > NEG / 2), 1.0, sel)
            score = jnp.where(hit, -3e38, score)
        sel_bf.append(jnp.concatenate([sel, jnp.zeros((LANES - n_blk, QT), F32)], axis=0).astype(BF16))

    def slc_body(kt, carry):
        m_prev, l_prev = carry
        k0 = pl.multiple_of(kt * TK_SLC, TK_SLC)
        kpos = k0 + lax.broadcasted_iota(jnp.int32, (TK_SLC, QT), 0)
        causal = kpos <= qt * QT + lax.broadcasted_iota(jnp.int32, (TK_SLC, QT), 1)
        ex = ext_ref[pl.ds(k0, TK_SLC), :]
        parts = []
        for gi in range(2):
            mk = _dot(ex, sel_bf[gi])
            parts += [jnp.where((mk > 0.5) & causal, 0.0, NEG)] * HEADS_PER_KV
        s = _dot(ks_ref[pl.ds(k0, TK_SLC), :].astype(BF16), qT) + jnp.concatenate(parts, axis=1)
        m_new = jnp.maximum(m_prev, jnp.max(s, axis=0, keepdims=True))
        alpha = jnp.exp2(m_prev - m_new)
        p = jnp.exp2(s - m_new)
        l_new = alpha * l_prev + jnp.sum(p, axis=0, keepdims=True)
        vt = _tile_t(vs_ref, k0, TK_SLC // LANES).astype(BF16)
        acc_ref[...] = alpha * acc_ref[...] + _dot(vt, p.astype(BF16))
        return m_new, l_new

    acc_ref[...] = jnp.zeros(acc_ref.shape, F32)
    n_slc_tiles = jnp.right_shift(qt, 2) + 1
    _, l_s = lax.fori_loop(0, n_slc_tiles, slc_body,
                           (jnp.full((1, COLS), NEG, F32), jnp.zeros((1, COLS), F32)))
    o_s = acc_ref[...] / l_s

    w0 = pl.multiple_of(jnp.maximum(qt - WINDOW // QT, 0) * QT, QT)
    d = (qt * QT + lax.broadcasted_iota(jnp.int32, (WIN_KEYS, QT), 1)
         - (w0 + lax.broadcasted_iota(jnp.int32, (WIN_KEYS, QT), 0)))
    sw = (_dot(kw_ref[pl.ds(w0, WIN_KEYS), :].astype(BF16), qT)
          + tile8(jnp.where((d >= 0) & (d < WINDOW), 0.0, NEG)))
    pw = jnp.exp2(sw - jnp.max(sw, axis=0, keepdims=True))
    vwt = _tile_t(vw_ref, w0, WIN_KEYS // LANES).astype(BF16)
    o_w = _dot(vwt, pw.astype(BF16)) / jnp.sum(pw, axis=0, keepdims=True)

    gs_ref[...] = jax.nn.sigmoid(gt_ref[...] + bg_ref[...]).T

    def gate_row(c):
        g8 = gs_ref[pl.ds(pl.multiple_of(c * N_HEADS + gp * 8, 8), 8), :]
        return jnp.concatenate([g8[hh:hh + 1, :] for hh in range(8)], axis=1)

    comb = gate_row(0) * o_c + gate_row(1) * o_s + gate_row(2) * o_w
    for m in range(4):
        r0 = (m // 2) * HEAD_DIM
        top = comb[r0:r0 + HEAD_DIM, (2 * m) * QT:(2 * m + 1) * QT]
        bot = comb[r0:r0 + HEAD_DIM, (2 * m + 1) * QT:(2 * m + 2) * QT]
        o_ref[:, m * LANES:(m + 1) * LANES] = jnp.concatenate([top, bot], axis=0).T


def _attn_consts(seq):
    n = np.arange(LANES)
    ci = n[:, None] * CMP_STRIDE
    sj = n[None, :] * SLC_BLOCK
    n_cmp = (seq - CMP_BLOCK) // CMP_STRIDE + 1
    n_slc = -(-seq // SLC_BLOCK)
    ov = ((ci < sj + SLC_BLOCK) & (ci + CMP_BLOCK > sj) & (n[:, None] < n_cmp) & (n[None, :] < n_slc))
    ex = (np.arange(seq)[None, :] // SLC_BLOCK) == n[:, None]
    return (jnp.asarray(ov.T.astype(np.float32), BF16), jnp.asarray(ex.T.astype(np.float32), BF16))


def _attn_prompt(q, kv4, kvw, kc, vc, gates, bgate, n_batch, seq):
    ovt, ext = _attn_consts(seq)
    nqt = seq // QT
    kspec = lambda col0: pl.BlockSpec((seq, LANES), lambda b, gp, t: (b, col0 + gp))
    cspec = pl.BlockSpec((None, None, 128, LANES), lambda b, gp, t: (b, gp, 0, 0))
    return pl.pallas_call(
        _attn_kernel,
        grid=(n_batch, 2, nqt),
        in_specs=[
            pl.BlockSpec((QT, 512), lambda b, gp, t: (b * nqt + t, gp)),
            kspec(4), kspec(6), kspec(0), kspec(2), cspec, cspec,
            pl.BlockSpec((QT, LANES), lambda b, gp, t: (b * nqt + t, 0)),
            pl.BlockSpec((1, LANES), lambda b, gp, t: (0, 0)),
            pl.BlockSpec(ovt.shape, lambda b, gp, t: (0, 0)),
            pl.BlockSpec(ext.shape, lambda b, gp, t: (0, 0)),
        ],
        out_specs=pl.BlockSpec((QT, 512), lambda b, gp, t: (b * nqt + t, gp)),
        out_shape=jax.ShapeDtypeStruct((n_batch * seq, 1024), F32),
        scratch_shapes=[pltpu.VMEM((LANES, COLS), F32), pltpu.VMEM((LANES, QT), F32)],
        compiler_params=_cparams(3),
        name="nsa_attn_prompt",
    )(q, kv4, kv4, kvw, kvw, kc, vc, gates, bgate, ovt, ext)


N_PAGES = 16
BPS = 2
SLAB_PITCH = 24


def _s1_kernel(layer, pt_ref, cache_ref, q_ref, wcat_ref, p_ref, w2_ref, c_ref, s1_ref, s2_ref,
               ov_ref, oc_ref, sel_ref, raw_ref, slab_ref, lhs_ref, sem_ref):
    step = pl.program_id(0)
    n_steps = pl.num_programs(0)

    def page_copy(st, slot, bi, p):
        return pltpu.make_async_copy(
            cache_ref.at[layer, pt_ref[st * BPS + bi, p], pl.ds(0, 2)],
            raw_ref.at[slot, bi, p],
            sem_ref.at[slot])

    def start_all(st, slot):
        for bi in range(BPS):
            for p in range(N_PAGES):
                page_copy(st, slot, bi, p).start()

    slot = step % 2

    @pl.when(step == 0)
    def _():
        start_all(0, 0)

    @pl.when(step + 1 < n_steps)
    def _():
        start_all(step + 1, 1 - slot)

    for bi in range(BPS):
        for p in range(N_PAGES):
            page_copy(step, slot, bi, p).wait()

    for bi in range(BPS):
        _s1_one(bi, slot, q_ref, wcat_ref, p_ref, w2_ref, c_ref, s1_ref, s2_ref, ov_ref, oc_ref, sel_ref,
                raw_ref, slab_ref, lhs_ref)


def _s1_one(bi, slot, q_ref, wcat_ref, p_ref, w2_ref, c_ref, s1_ref, s2_ref, ov_ref, oc_ref, sel_ref,
            raw_ref, slab_ref, lhs_ref):
    chunks = PAGE_SIZE // CMP_STRIDE
    for p in range(N_PAGES):
        for cb in range(4):
            kind, pr = cb // 2, cb % 2
            t = raw_ref[slot, bi, p, kind, pr * LANES:(pr + 1) * LANES, :].T
            for m in range(chunks):
                r0 = (p * chunks + m) * SLAB_PITCH
                slab_ref[bi, cb, r0:r0 + CMP_STRIDE, :] = t[m * CMP_STRIDE:(m + 1) * CMP_STRIDE]

    def load(l, cb):
        return slab_ref[bi, cb, pl.ds(l, 128, stride=SLAB_PITCH), :]

    outs = []
    for kind in range(2):
        _fill_lhs_strided(load, lhs_ref.at[bi], kind)
        outs.append(_compress_mlp(lhs_ref.at[bi], wcat_ref, p_ref, w2_ref, kind))
    ko, vo = outs
    kc = jnp.concatenate([_rope_slab(k, c_ref[...], s1_ref[...], s2_ref[...]) for k in ko], axis=1)
    vc = jnp.concatenate(vo, axis=1)

    qp = (q_ref[bi] * SCALE).astype(BF16)
    sc = _dot_nt(qp, kc.astype(BF16))
    lane = lax.broadcasted_iota(jnp.int32, (N_HEADS, LANES), 1)
    ok = lane < (LANES - 1)
    sc = jnp.where(ok, sc, NEG)
    mc = jnp.max(sc, axis=1, keepdims=True)
    ec = jnp.where(ok, jnp.exp(sc - mc), 0.0)
    pc = ec / jnp.sum(ec, axis=1, keepdims=True)
    oc_ref[bi] = _dot(pc.astype(BF16), vc.astype(BF16))

    imp_h = _split_dot(pc, ov_ref[...])
    rowg = jnp.right_shift(lax.broadcasted_iota(jnp.int32, (N_HEADS, LANES), 0), 2)
    row8 = lax.broadcasted_iota(jnp.int32, (8, LANES), 0)
    imp = jnp.zeros((8, LANES), F32)
    for g in range(N_KV):
        ig = jnp.sum(jnp.where(rowg == g, imp_h, 0.0), axis=0, keepdims=True)
        imp = jnp.where(row8 == g, ig, imp)
    lane8 = lax.broadcasted_iota(jnp.int32, (8, LANES), 1)
    jf = lane8.astype(F32)
    last = 2048 // SLC_BLOCK
    score = jnp.where((lane8 >= 1) & (lane8 <= last - 2), imp, NEG)
    picks = jnp.where(lane8 == 6, float(last - 1), 0.0)
    for k in range(TOP_N - 3):
        mx = jnp.max(score, axis=1, keepdims=True)
        first = jnp.min(jnp.where(score == mx, jf, 1e9), axis=1, keepdims=True)
        picks = jnp.where(lane8 == k, first, picks)
        score = jnp.where(jf == first, -3e38, score)
    sel_ref[bi] = picks.astype(jnp.int32)


def _s1(layer, page_table, cache, q_exp, wcat, pflat, w2bd, ctabs, ov):
    nb = q_exp.shape[0]
    full = lambda a: pl.BlockSpec(a.shape, lambda b, pt: (0,) * a.ndim)
    grid_spec = pltpu.PrefetchScalarGridSpec(
        num_scalar_prefetch=1,
        grid=(nb // BPS,),
        in_specs=[pl.BlockSpec(memory_space=pl.ANY),
                  pl.BlockSpec((BPS, N_HEADS, 256), lambda b, pt: (b, 0, 0)),
                  full(wcat), full(pflat), full(w2bd), full(ctabs[0]), full(ctabs[1]), full(ctabs[2]),
                  full(ov)],
        out_specs=[pl.BlockSpec((BPS, N_HEADS, 256), lambda b, pt: (b, 0, 0)),
                   pl.BlockSpec((BPS, 8, LANES), lambda b, pt: (b, 0, 0))],
        scratch_shapes=[pltpu.VMEM((2, BPS, N_PAGES, 2, 256, PAGE_SIZE), F32),
                        pltpu.VMEM((BPS, 4, 128 * SLAB_PITCH, LANES), F32),
                        pltpu.VMEM((BPS, 512, 1024), BF16),
                        pltpu.SemaphoreType.DMA((2,))],
    )
    return pl.pallas_call(
        functools.partial(_s1_kernel, layer),
        grid_spec=grid_spec,
        out_shape=[jax.ShapeDtypeStruct((nb, N_HEADS, 256), F32),
                   jax.ShapeDtypeStruct((nb, 8, LANES), jnp.int32)],
        compiler_params=_cparams(1),
        name="nsa_sample_s1",
    )(page_table, cache, q_exp, wcat, pflat, w2bd, *ctabs, ov)


N_HIST = TOP_N - 1
KSEL = N_HIST * PAGE_SIZE


def _s2_kernel(layer, has_prev, pt_ref, sel_ref, cache_ref, win_ref, q_ref, oc_ref, kvn_ref, kwn_ref,
               gt_ref, bg_ref, hsel_ref, *rest):
    o_ref, nwin_ref, kbuf_ref, vbuf_ref, sem_ref = rest[1:] if has_prev else rest
    if has_prev:
        nwin_ref[0:layer] = rest[0][...]
    step = pl.program_id(0)
    n_steps = pl.num_programs(0)

    def copies(st, slot, bi, g, k):
        bb = st * BPS + bi
        blk = sel_ref[(bb * N_KV + g) * 8 + k]
        page = pt_ref[bb, jnp.right_shift(blk, 1)]
        rows = pl.ds(g * HEAD_DIM, HEAD_DIM)
        dst = pl.ds(k * PAGE_SIZE, PAGE_SIZE)
        return (pltpu.make_async_copy(cache_ref.at[layer, page, 2, rows, :],
                                      kbuf_ref.at[slot, bi, g, :, dst], sem_ref.at[slot]),
                pltpu.make_async_copy(cache_ref.at[layer, page, 3, rows, :],
                                      vbuf_ref.at[slot, bi, g, :, dst], sem_ref.at[slot]))

    def for_all(st, slot, fn):
        for bi in range(BPS):
            for g in range(N_KV):
                for k in range(N_HIST):
                    ck, cv = copies(st, slot, bi, g, k)
                    fn(ck)
                    fn(cv)

    slot = step % 2

    @pl.when(step == 0)
    def _():
        for_all(0, 0, lambda c: c.start())

    @pl.when(step + 1 < n_steps)
    def _():
        for_all(step + 1, 1 - slot, lambda c: c.start())

    for_all(step, slot, lambda c: c.wait())

    for bi in range(BPS):
        _s2_one(layer, bi, step * BPS + bi, slot, sel_ref, win_ref, q_ref, oc_ref, kvn_ref, kwn_ref,
                gt_ref, bg_ref, hsel_ref, o_ref, nwin_ref, kbuf_ref, vbuf_ref)


def _s2_one(layer, bi, b, slot, sel_ref, win_ref, q_ref, oc_ref, kvn_ref, kwn_ref, gt_ref, bg_ref,
            hsel_ref, o_ref, nwin_ref, kbuf_ref, vbuf_ref):
    qf = q_ref[bi] * SCALE
    qp = qf.astype(BF16)
    q16 = (qf[:, 0:64] + qf[:, 64:128] + qf[:, 128:192] + qf[:, 192:256]).astype(BF16)
    lane_k = lax.broadcasted_iota(jnp.int32, (N_HEADS, KSEL), 1)
    rowg_k = jnp.right_shift(lax.broadcasted_iota(jnp.int32, (N_HEADS, KSEL), 0), 2)
    rowg_d = jnp.right_shift(lax.broadcasted_iota(jnp.int32, (N_HEADS, HEAD_DIM), 0), 2)
    tile_k = jnp.right_shift(lane_k, 7)
    half_k = jnp.bitwise_and(jnp.right_shift(lane_k, 6), 1)

    def attend(s_hist, s_new, pv_hist, v_new):
        m = jnp.maximum(jnp.max(s_hist, axis=1, keepdims=True), s_new)
        e = jnp.exp(s_hist - m)
        en = jnp.exp(s_new - m)
        l = jnp.sum(e, axis=1, keepdims=True) + en
        return (pv_hist(e.astype(BF16)) + en * v_new) / l

    kn = kvn_ref[bi]
    s_hist = jnp.zeros((N_HEADS, KSEL), F32)
    v16 = jnp.zeros((N_HEADS, HEAD_DIM), F32)
    for g in range(N_KV):
        sg = _dot(q16, kbuf_ref[slot, bi, g].astype(BF16))
        want = jnp.zeros((N_HEADS, KSEL), jnp.int32)
        for k in range(N_HIST):
            half = jnp.bitwise_and(sel_ref[(b * N_KV + g) * 8 + k], 1)
            want = jnp.where(tile_k == k, half, want)
        s_hist = jnp.where(rowg_k == g, jnp.where(half_k == want, sg, NEG), s_hist)
        v16 = jnp.where(rowg_d == g, kn[:, 768 + g * HEAD_DIM:768 + (g + 1) * HEAD_DIM], v16)
    s_new = jnp.sum(qf * kn[:, 512:768], axis=1, keepdims=True)

    def pv_slc(e):
        o = jnp.zeros((N_HEADS, HEAD_DIM), F32)
        for g in range(N_KV):
            o = jnp.where(rowg_d == g, _dot_nt(e, vbuf_ref[slot, bi, g].astype(BF16)), o)
        return o

    o_s = attend(s_hist, s_new, pv_slc, v16)
    o_s = jnp.concatenate([o_s] * N_KV, axis=1)

    wn = kwn_ref[bi]
    s_w = _dot(qp, win_ref[bi, 0].astype(BF16))
    colw = lax.broadcasted_iota(jnp.int32, (N_HEADS, WINDOW), 1)
    s_w = jnp.where(colw >= 1, s_w, NEG)
    s_wn = jnp.sum(qf * wn[:, 0:256], axis=1, keepdims=True)
    vw = win_ref[bi, 1].astype(BF16)
    o_w = attend(s_w, s_wn, lambda e: _dot_nt(e, vw), wn[:, 256:512])

    lane = lax.broadcasted_iota(jnp.int32, (LANES, LANES), 1)
    for c in range(2):
        for rs in range(2):
            rows = slice(rs * LANES, (rs + 1) * LANES)
            new = wn[:, c * 256 + rs * LANES:c * 256 + (rs + 1) * LANES]
            col = jnp.broadcast_to(new, (LANES, LANES)).T
            tiles = [pltpu.roll(win_ref[bi, c, rows, j * LANES:(j + 1) * LANES], LANES - 1, 1)
                     for j in range(WINDOW // LANES)]
            tiles.append(col)
            for j in range(WINDOW // LANES):
                nwin_ref[layer, bi, c, rows, j * LANES:(j + 1) * LANES] = jnp.where(
                    lane == LANES - 1, tiles[j + 1], tiles[j])

    gsig = jax.nn.sigmoid(gt_ref[bi] + bg_ref[...])
    gcol = lambda c: jnp.sum(hsel_ref[c] * gsig, axis=1, keepdims=True)
    o_ref[bi] = gcol(0) * oc_ref[bi] + gcol(1) * o_s + gcol(2) * o_w


def _s2(layer, page_table, sel_flat, cache, win, q_exp, oc, kv4s, kvws, gates, bgate, hsel, prev):
    nb = q_exp.shape[0]
    hq = pl.BlockSpec((BPS, N_HEADS, 256), lambda b, pt, sl: (b, 0, 0))
    row = lambda w: pl.BlockSpec((BPS, 1, w), lambda b, pt, sl: (b, 0, 0))
    wspec = pl.BlockSpec((None, BPS, 2, 256, WINDOW), lambda b, pt, sl: (layer, b, 0, 0, 0))
    stack = lambda n: pl.BlockSpec((n, BPS, 2, 256, WINDOW), lambda b, pt, sl: (0, b, 0, 0, 0))
    in_specs = [pl.BlockSpec(memory_space=pl.ANY), wspec,
                hq, hq, row(1024), row(512), row(LANES),
                pl.BlockSpec((1, LANES), lambda b, pt, sl: (0, 0)),
                pl.BlockSpec(hsel.shape, lambda b, pt, sl: (0, 0, 0))]
    args = [page_table, sel_flat, cache, win, q_exp, oc, kv4s, kvws, gates, bgate, hsel]
    if prev is not None:
        in_specs.append(stack(layer))
        args.append(prev)
    grid_spec = pltpu.PrefetchScalarGridSpec(
        num_scalar_prefetch=2,
        grid=(nb // BPS,),
        in_specs=in_specs,
        out_specs=[hq, stack(layer + 1)],
        scratch_shapes=[pltpu.VMEM((2, BPS, N_KV, HEAD_DIM, KSEL), F32),
                        pltpu.VMEM((2, BPS, N_KV, HEAD_DIM, KSEL), F32),
                        pltpu.SemaphoreType.DMA((2,))],
    )
    return pl.pallas_call(
        functools.partial(_s2_kernel, layer, prev is not None),
        grid_spec=grid_spec,
        out_shape=[jax.ShapeDtypeStruct((nb, N_HEADS, 256), F32),
                   jax.ShapeDtypeStruct((layer + 1, nb, 2, 256, WINDOW), F32)],
        compiler_params=_cparams(1),
        name="nsa_sample_s2",
    )(*args)


def _sg_in_kernel(x_ref, g_ref, w_ref, lg_ref, lb_ref, u_ref, v_ref):
    xb = _rmsnorm(x_ref[...], g_ref[...]).astype(BF16)
    u_ref[...] = jax.nn.gelu(_dot(xb, w_ref[:, 0:1024]))
    v = jax.nn.gelu(_dot(xb, w_ref[:, 1024:2048]))
    mu = jnp.mean(v, axis=-1, keepdims=True)
    var = jnp.mean(jnp.square(v - mu), axis=-1, keepdims=True)
    v_ref[...] = (v - mu) * lax.rsqrt(var + EPS) * lg_ref[...] + lb_ref[...]


def _sg_in(x, g, w, lg, lb, tm):
    m = x.shape[0]
    full = lambda a: pl.BlockSpec(a.shape, lambda i: (0,) * a.ndim, pipeline_mode=pl.Buffered(1))
    row = pl.BlockSpec((tm, 1024), lambda i: (i, 0))
    return pl.pallas_call(
        _sg_in_kernel,
        grid=(m // tm,),
        in_specs=[row, full(g), full(w), full(lg), full(lb)],
        out_specs=[row, row],
        out_shape=[jax.ShapeDtypeStruct((m, 1024), F32)] * 2,
        compiler_params=_cparams(1),
        name="sg_in",
    )(x, g, w, lg, lb)


FF_CHUNK = 1024


def _ffn_tail(final, x1, g_ref, w1_ref, w2_ref, gfin_ref, o_ref):
    xb = _rmsnorm(x1, g_ref[...]).astype(BF16)
    acc = x1
    for c in range(D_FF // FF_CHUNK):
        sl = slice(c * FF_CHUNK, (c + 1) * FF_CHUNK)
        h = jnp.maximum(_dot(xb, w1_ref[:, sl]), 0.0)
        acc = acc + _dot((h * h).astype(BF16), w2_ref[sl, :])
    o_ref[...] = _rmsnorm(acc, gfin_ref[...]) if final else acc


def _post_kernel(final, x_ref, a_ref, wo_ref, g_ref, w1_ref, w2_ref, gfin_ref, o_ref):
    x1 = x_ref[...] + _dot(a_ref[...].astype(BF16), wo_ref[...])
    _ffn_tail(final, x1, g_ref, w1_ref, w2_ref, gfin_ref, o_ref)


def _post_sg_prompt_kernel(final, x_ref, u_ref, v_ref, ws_ref, bs_ref, wo_ref, g_ref, w1_ref, w2_ref,
                           gfin_ref, o_ref, a_ref):
    r = lax.broadcasted_iota(jnp.int32, (CHUNK, CHUNK), 0)
    c = lax.broadcasted_iota(jnp.int32, (CHUNK, CHUNK), 1)
    tril = r >= c
    for g in range(SG_GROUPS):
        w = jnp.where(tril, ws_ref[g], 0.0).astype(BF16)
        gl = slice(g * 128, (g + 1) * 128)
        for ch in range(x_ref.shape[0] // CHUNK):
            rs = slice(ch * CHUNK, (ch + 1) * CHUNK)
            s = _dot(w, v_ref[rs, gl].astype(BF16)) + bs_ref[:, gl]
            a_ref[rs, gl] = (u_ref[rs, gl] * s).astype(BF16)
    x1 = x_ref[...] + _dot(a_ref[...], wo_ref[...])
    _ffn_tail(final, x1, g_ref, w1_ref, w2_ref, gfin_ref, o_ref)


def _post_sg_sample_kernel(final, x_ref, u_ref, v_ref, w0_ref, b0_ref, wo_ref, g_ref, w1_ref, w2_ref,
                           gfin_ref, o_ref):
    a = u_ref[...] * (v_ref[...] * w0_ref[...] + b0_ref[...])
    x1 = x_ref[...] + _dot(a.astype(BF16), wo_ref[...])
    _ffn_tail(final, x1, g_ref, w1_ref, w2_ref, gfin_ref, o_ref)


def _post_call(kern, name, final, x, row_ins, full_ins, tm, scratch=()):
    m = x.shape[0]
    full = lambda a: pl.BlockSpec(a.shape, lambda i: (0,) * a.ndim, pipeline_mode=pl.Buffered(1))
    row = lambda a: pl.BlockSpec((tm, a.shape[1]), lambda i: (i, 0))
    return pl.pallas_call(
        functools.partial(kern, final),
        grid=(m // tm,),
        in_specs=[row(a) for a in row_ins] + [full(a) for a in full_ins],
        out_specs=pl.BlockSpec((tm, D_MODEL), lambda i: (i, 0)),
        out_shape=jax.ShapeDtypeStruct((m, D_MODEL), F32),
        scratch_shapes=list(scratch),
        compiler_params=_cparams(1),
        name=name,
    )(*row_ins, *full_ins)


def _expand_heads(w, axis):
    parts = []
    for h in range(N_HEADS):
        s = h // HEADS_PER_KV
        pad = [(0, 0), (0, 0)]
        pad[axis] = (s * HEAD_DIM, (N_KV - 1 - s) * HEAD_DIM)
        parts.append(jnp.pad(lax.slice_in_dim(w, h * HEAD_DIM, (h + 1) * HEAD_DIM, axis=axis), pad))
    return jnp.concatenate(parts, axis=axis)


def kernel(x_prompt, x_sample, cache_nsa_kv, state_nsa_win, page_table, g_mix, g_ffn, g_final,
           nsa_w_in, nsa_b_gate, nsa_cmp_pos, nsa_cmp_w1, nsa_cmp_w2, nsa_w_out,
           sg_w_in, sg_ln_g, sg_ln_b, sg_w_spatial, sg_b_spatial, sg_w_out, ffn_w1, ffn_w2):
    nb, seq, _ = x_prompt.shape
    nd = x_sample.shape[0]
    depth = g_mix.shape[0]
    n_pool = cache_nsa_kv.shape[1]
    past = page_table.shape[1] * PAGE_SIZE
    tm_p = 512

    xp = x_prompt.reshape(nb * seq, D_MODEL)
    xs = x_sample.reshape(nd, D_MODEL)
    cache = cache_nsa_kv.transpose(0, 1, 3, 4, 5, 2).reshape(cache_nsa_kv.shape[0], n_pool, 4, 256,
                                                              PAGE_SIZE)
    win = state_nsa_win.transpose(0, 1, 3, 4, 5, 2).reshape(state_nsa_win.shape[0], nd, 2, 256, WINDOW)
    nwin = None

    tabs_p = _rope_tables(jnp.arange(seq, dtype=jnp.int32))
    tabs_s = _rope_tables(jnp.full((nd,), past, jnp.int32))
    tabs_c = _rope_tables(jnp.arange(LANES, dtype=jnp.int32) * CMP_STRIDE + CMP_BLOCK - 1)

    nidx = np.arange(LANES)
    ci = nidx[:, None] * CMP_STRIDE
    sj = nidx[None, :] * SLC_BLOCK
    ov_s = ((ci < sj + SLC_BLOCK) & (ci + CMP_BLOCK > sj) & (nidx[:, None] < LANES - 1))
    ov_s = jnp.asarray(ov_s.astype(np.float32), BF16)
    hsel = np.zeros((3, N_HEADS, LANES), np.float32)
    for c in range(3):
        hsel[c, np.arange(N_HEADS), c * N_HEADS + np.arange(N_HEADS)] = 1.0
    hsel = jnp.asarray(hsel)

    gfin = g_final.reshape(1, D_MODEL)
    kvt, kwt = None, None
    kv_s, v_s = [], []
    for i in range(depth):
        j = i // 2
        final = i == depth - 1
        gm = g_mix[i].reshape(1, D_MODEL)
        gf = g_ffn[i].reshape(1, D_MODEL)
        w1 = ffn_w1[i].astype(BF16)
        w2 = ffn_w2[i].astype(BF16)
        if i % 2 == 0:
            w_in = nsa_w_in[j]
            wq = w_in[:, 0:1024].astype(BF16)
            wq_exp = _expand_heads(wq, 1)
            wkv = w_in[:, 1024:2048].astype(BF16)
            wkw = w_in[:, 2048:2560].astype(BF16)
            wg = jnp.pad(w_in[:, 2560:2608], ((0, 0), (0, LANES - 48))).astype(BF16)
            bgate = jnp.pad(nsa_b_gate[j], (0, LANES - 48)).reshape(1, LANES)
            w1c = nsa_cmp_w1[j]
            wcat = jnp.concatenate([w1c[:, :16].reshape(2, 1024, CMP_HID),
                                    w1c[:, 16:].reshape(2, 1024, CMP_HID)], axis=2).astype(BF16)
            pe = nsa_cmp_pos[j]
            pflat = jnp.concatenate([pe[:, :16].reshape(2, 1, 1024), pe[:, 16:].reshape(2, 1, 1024),
                                     jnp.zeros((2, 6, 1024), F32)], axis=1).astype(BF16)
            w2c = nsa_cmp_w2[j]
            zc = jnp.zeros_like(w2c)
            w2bd = jnp.concatenate([jnp.concatenate([w2c, zc], axis=2),
                                    jnp.concatenate([zc, w2c], axis=2)], axis=1).astype(BF16)
            wo = nsa_w_out[j].astype(BF16)
            wo_exp = _expand_heads(wo, 0)

            q_p, kv4_p, kvw_p, gt_p, kvt, kwt = _proj(xp, gm, tabs_p, seq // tm_p, wq, wkv, wkw, wg, tm_p,
                                                      stack=(j, nb, seq, kvt, kwt))
            q_s, kv4_s, kvw_s, gt_s = _proj(xs, gm, tabs_s, 1, wq_exp, wkv, wkw, wg, nd)

            kc, vc = _cmp_prompt(kv4_p, nb, seq, wcat, pflat, w2bd, tabs_c)
            a_p = _attn_prompt(q_p, kv4_p, kvw_p, kc, vc, gt_p, bgate, nb, seq)

            q_exp = q_s.reshape(nd, N_HEADS, 256)
            oc_s, sel = _s1(j, page_table, cache, q_exp, wcat, pflat, w2bd, tabs_c, ov_s)
            sel_flat = sel[:, :N_KV, :8].reshape(-1)
            a_s, nwin = _s2(j, page_table, sel_flat, cache, win, q_exp, oc_s,
                            kv4_s.reshape(nd, 1, 1024), kvw_s.reshape(nd, 1, 512),
                            gt_s.reshape(nd, 1, LANES), bgate, hsel, nwin)

            xp = _post_call(_post_kernel, "post_nsa", final, xp, [xp, a_p], [wo, gf, w1, w2, gfin], tm_p)
            xs = _post_call(_post_kernel, "post_nsa", final, xs, [xs, a_s.reshape(nd, 4096)],
                            [wo_exp, gf, w1, w2, gfin], nd)
            kv_s.append(kv4_s.reshape(nd, 1, 4, N_KV, HEAD_DIM))
        else:
            w_in = sg_w_in[j].astype(BF16)
            lg = sg_ln_g[j].reshape(1, 1024)
            lb = sg_ln_b[j].reshape(1, 1024)
            ws = sg_w_spatial[j]
            bs = sg_b_spatial[j]
            bs_exp = jnp.repeat(bs.T, CHUNK, axis=1)
            w0 = jnp.repeat(ws[:, 0, 0], CHUNK).reshape(1, 1024)
            b0 = bs_exp[0:1]
            wo = sg_w_out[j].astype(BF16)

            u_p, v_p = _sg_in(xp, gm, w_in, lg, lb, tm_p)
            u_s, vv_s = _sg_in(xs, gm, w_in, lg, lb, nd)
            xp = _post_call(_post_sg_prompt_kernel, "post_sg_prompt", final, xp, [xp, u_p, v_p],
                            [ws, bs_exp, wo, gf, w1, w2, gfin], tm_p,
                            scratch=[pltpu.VMEM((tm_p, 1024), BF16)])
            xs = _post_call(_post_sg_sample_kernel, "post_sg_sample", final, xs, [xs, u_s, vv_s],
                            [w0, b0, wo, gf, w1, w2, gfin], nd)
            v_s.append(vv_s.reshape(nd, 1, 1024))

    y_prompt = xp.reshape(nb, seq, D_MODEL)
    y_sample = xs.reshape(nd, 1, D_MODEL)
    n_l = kvt.shape[0]
    to_rows = lambda a, kinds, rows: a.reshape(n_l, a.shape[1], kinds, N_KV, HEAD_DIM, rows).transpose(
        0, 1, 5, 2, 3, 4)
    return (y_prompt, y_sample, to_rows(kvt, 4, seq), to_rows(kwt, 2, WINDOW), jnp.stack(kv_s),
            to_rows(nwin, 2, WINDOW), jnp.stack(v_s))
```

```python
import functools

import numpy as np
import jax
import jax.numpy as jnp
from jax import lax
from jax.experimental import pallas as pl
from jax.experimental.pallas import tpu as pltpu

F32 = jnp.float32
BF16 = jnp.bfloat16

D_MODEL = 1024
N_HEADS = 16
HEAD_DIM = 64
N_KV = 4
HEADS_PER_KV = 4
ROPE_DIM = 16
ROPE_THETA = 500000.0
CMP_BLOCK = 32
CMP_STRIDE = 16
CMP_HID = 128
SLC_BLOCK = 64
TOP_N = 8
WINDOW = 512
PAGE_SIZE = 128
CHUNK = 128
SG_GROUPS = 8
D_FF = 4096
EPS = 1e-6
NEG = -1e30
FORCE = 1e6
SCALE = HEAD_DIM ** -0.5
LOG2E = 1.4426950408889634

LANES = 128
VMEM_LIMIT = 56 * 1024 * 1024


def _cparams(n_axes):
    return pltpu.CompilerParams(dimension_semantics=("arbitrary",) * n_axes,
                                vmem_limit_bytes=VMEM_LIMIT)


def _dot(a, b):
    return jnp.dot(a, b, preferred_element_type=F32)


def _dot_nt(a, b):
    return lax.dot_general(a, b, (((1,), (1,)), ((), ())), preferred_element_type=F32)


def _split_dot(a, b):
    hi = a.astype(BF16)
    lo = (a - hi.astype(F32)).astype(BF16)
    return _dot(hi, b) + _dot(lo, b)


def _rmsnorm(x, g):
    return x * lax.rsqrt(jnp.mean(x * x, axis=-1, keepdims=True) + EPS) * g


def _rope_slab(x, c, s1, s2):
    return x * c + pltpu.roll(x, LANES - 8, 1) * s1 + pltpu.roll(x, 8, 1) * s2


def _rope_tables(pos):
    half = ROPE_DIM // 2
    inv = jnp.power(jnp.float32(ROPE_THETA), -jnp.arange(half, dtype=F32) / half)
    ang = pos.astype(F32)[:, None] * inv[None, :]
    cos, sin = jnp.cos(ang), jnp.sin(ang)
    n = pos.shape[0]
    z = lambda w: jnp.zeros((n, w), F32)
    c = jnp.concatenate([cos, cos, jnp.ones((n, HEAD_DIM - ROPE_DIM), F32)], axis=1)
    s1 = jnp.concatenate([-sin, z(HEAD_DIM - half)], axis=1)
    s2 = jnp.concatenate([z(half), sin, z(HEAD_DIM - ROPE_DIM)], axis=1)
    t2 = lambda a: jnp.concatenate([a, a], axis=1)
    return t2(c), t2(s1), t2(s2)


def _proj_kernel(stack_t, layer, tiles_per_seq, x_ref, g_ref, c_ref, s1_ref, s2_ref, wq_ref, wkv_ref,
                 wkw_ref, wg_ref, *rest):
    if stack_t and layer > 0:
        pkv_ref, pkw_ref = rest[:2]
        rest = rest[2:]
    q_ref, kv_ref, kw_ref, gt_ref = rest[:4]
    xb = _rmsnorm(x_ref[...], g_ref[...]).astype(BF16)
    c, s1, s2 = c_ref[...], s1_ref[...], s2_ref[...]
    q = _dot(xb, wq_ref[...])
    for j in range(q.shape[1] // LANES):
        sl = slice(j * LANES, (j + 1) * LANES)
        q_ref[:, sl] = _rope_slab(q[:, sl], c, s1, s2)
    kv = _dot(xb, wkv_ref[...])
    kv_ref[:, 0:512] = kv[:, 0:512]
    for j in (4, 5):
        sl = slice(j * LANES, (j + 1) * LANES)
        kv_ref[:, sl] = _rope_slab(kv[:, sl], c, s1, s2)
    kv_ref[:, 768:1024] = kv[:, 768:1024]
    kw = _dot(xb, wkw_ref[...])
    for j in (0, 1):
        sl = slice(j * LANES, (j + 1) * LANES)
        kw_ref[:, sl] = _rope_slab(kw[:, sl], c, s1, s2)
    kw_ref[:, 256:512] = kw[:, 256:512]
    gt_ref[...] = _dot(xb, wg_ref[...])
    if not stack_t:
        return
    kvt_ref, kwt_ref = rest[4:6]
    tm = x_ref.shape[0]

    def put_t(dst_ref, src_ref, width):
        for r in range(tm // LANES):
            for cb in range(width // LANES):
                dst_ref[layer, cb * LANES:(cb + 1) * LANES, r * LANES:(r + 1) * LANES] = (
                    src_ref[r * LANES:(r + 1) * LANES, cb * LANES:(cb + 1) * LANES].T)

    if layer > 0:
        kvt_ref[0:layer] = pkv_ref[...]
    put_t(kvt_ref, kv_ref, 1024)

    @pl.when(pl.program_id(0) % tiles_per_seq == tiles_per_seq - 1)
    def _():
        if layer > 0:
            kwt_ref[0:layer] = pkw_ref[...]
        put_t(kwt_ref, kw_ref, 512)


def _proj(x, g, tabs, tab_period_blocks, wq, wkv, wkw, wg, tm, stack=None):
    m = x.shape[0]
    nq = wq.shape[1]
    full = lambda a: pl.BlockSpec(a.shape, lambda i: (0,) * a.ndim, pipeline_mode=pl.Buffered(1))
    tab_spec = pl.BlockSpec((tm, LANES), lambda i: (i % tab_period_blocks, 0))
    row = lambda w: pl.BlockSpec((tm, w), lambda i: (i, 0))
    in_specs = [row(D_MODEL), full(g), tab_spec, tab_spec, tab_spec,
                full(wq), full(wkv), full(wkw), full(wg)]
    args = [x, g, *tabs, wq, wkv, wkw, wg]
    out_specs = [row(nq), row(1024), row(512), row(LANES)]
    out_shape = [jax.ShapeDtypeStruct((m, nq), F32), jax.ShapeDtypeStruct((m, 1024), F32),
                 jax.ShapeDtypeStruct((m, 512), F32), jax.ShapeDtypeStruct((m, LANES), F32)]
    layer, tps = 0, 1
    if stack is not None:
        layer, n_batch, seq, prev_kvt, prev_kwt = stack
        assert tm == WINDOW and seq % tm == 0
        tps = seq // tm
        kvt = lambda n: pl.BlockSpec((n, None, 1024, tm), lambda i: (0, i // tps, 0, i % tps))
        kwt = lambda n: pl.BlockSpec((n, None, 512, WINDOW), lambda i: (0, i // tps, 0, 0))
        if layer > 0:
            in_specs += [kvt(layer), kwt(layer)]
            args += [prev_kvt, prev_kwt]
        out_specs += [kvt(layer + 1), kwt(layer + 1)]
        out_shape += [jax.ShapeDtypeStruct((layer + 1, n_batch, 1024, seq), F32),
                      jax.ShapeDtypeStruct((layer + 1, n_batch, 512, WINDOW), F32)]
    return pl.pallas_call(
        functools.partial(_proj_kernel, stack is not None, layer, tps),
        grid=(m // tm,),
        in_specs=in_specs,
        out_specs=out_specs,
        out_shape=out_shape,
        compiler_params=_cparams(1),
        name="nsa_proj",
    )(*args)


def _fill_lhs_strided(load, lhs_ref, kv):
    lane = lax.broadcasted_iota(jnp.int32, (128, LANES), 1)
    low = lane < 64
    for v in range(2):
        col = kv * 2 + v
        for lp in range(8):
            xe = load(2 * lp, col)
            xo = load(2 * lp + 1, col)
            re = pltpu.roll(xe, 64, 1)
            ro = pltpu.roll(xo, 64, 1)
            dst = slice(lp * LANES, (lp + 1) * LANES)
            lhs_ref[(2 * v) * 128:(2 * v + 1) * 128, dst] = jnp.where(low, xe, ro).astype(BF16)
            lhs_ref[(2 * v + 1) * 128:(2 * v + 2) * 128, dst] = jnp.where(low, re, xo).astype(BF16)


def _compress_mlp(lhs_ref, wcat_ref, p_ref, w2_ref, kv):
    w = wcat_ref[kv]
    c = _dot(lhs_ref[...], w)
    pb = _dot(p_ref[kv], w)
    bias = pb[0:1, 0:128] + pb[1:2, 128:256]
    hids = []
    for g in range(N_KV):
        lo = c[g * 128:(g + 1) * 128, 0:128]
        hi = c[g * 128:(g + 1) * 128, 128:256]
        pre = lo + pltpu.roll(hi, 127, 0) + bias
        hids.append(jax.nn.gelu(pre).astype(BF16))
    outs = []
    for pr in range(2):
        hc = jnp.concatenate([hids[2 * pr], hids[2 * pr + 1]], axis=1)
        outs.append(_dot(hc, w2_ref[kv]))
    return outs


def _cmp_prompt_kernel(x0_ref, x1_ref, x2_ref, x3_ref, wcat_ref, p_ref, w2_ref, c_ref, s1_ref, s2_ref,
                       kc_ref, vc_ref, lhs_ref):
    xs = (x0_ref, x1_ref, x2_ref, x3_ref)

    def load(l, cb):
        return xs[cb][pl.ds(l, 128, stride=CMP_STRIDE), :]
    outs = []
    for kv in range(2):
        _fill_lhs_strided(load, lhs_ref, kv)
        outs.append(_compress_mlp(lhs_ref, wcat_ref, p_ref, w2_ref, kv))
    ko, vo = outs
    for pr in range(2):
        kc_ref[pr] = _rope_slab(ko[pr], c_ref[...], s1_ref[...], s2_ref[...])
        vc_ref[pr] = vo[pr]


def _cmp_prompt(kv4, n_batch, seq, wcat, pflat, w2bd, ctabs):
    full = lambda a: pl.BlockSpec(a.shape, lambda b: (0,) * a.ndim)
    out = pl.BlockSpec((None, 2, 128, LANES), lambda b: (b, 0, 0, 0))
    return pl.pallas_call(
        _cmp_prompt_kernel,
        grid=(n_batch,),
        in_specs=[pl.BlockSpec((seq, LANES), functools.partial(lambda cb, b: (b, cb), cb))
                  for cb in range(4)] + [full(wcat), full(pflat), full(w2bd),
                  full(ctabs[0]), full(ctabs[1]), full(ctabs[2])],
        out_specs=[out, out],
        out_shape=[jax.ShapeDtypeStruct((n_batch, 2, 128, LANES), F32)] * 2,
        scratch_shapes=[pltpu.VMEM((512, 1024), BF16)],
        compiler_params=_cparams(1),
        name="nsa_cmp_prompt",
    )(kv4, kv4, kv4, kv4, wcat, pflat, w2bd, *ctabs)


QT = 256
COLS = 2 * HEADS_PER_KV * QT
TK_SLC = 512
WIN_KEYS = WINDOW + QT


def _tile_t(ref, r0, n_tiles):
    return jnp.concatenate([ref[pl.ds(r0 + i * LANES, LANES), :].T for i in range(n_tiles)], axis=1)


def _attn_kernel(q_ref, ks_ref, vs_ref, kw_ref, vw_ref, kc_ref, vc_ref, gt_ref, bg_ref,
                 ovt_ref, ext_ref, o_ref, acc_ref, gs_ref):
    gp = pl.program_id(1)
    qt = pl.program_id(2)

    zeros64 = jnp.zeros((HEAD_DIM, QT), F32)
    blocks = []
    for m in range(4):
        t = q_ref[:, m * LANES:(m + 1) * LANES].T * (SCALE * LOG2E)
        for par in range(2):
            dims = t[par * HEAD_DIM:(par + 1) * HEAD_DIM]
            blocks.append(jnp.concatenate([dims, zeros64] if m < 2 else [zeros64, dims], axis=0))
    qT = jnp.concatenate(blocks, axis=1).astype(BF16)

    def tile8(x):
        return jnp.concatenate([x] * 8, axis=1)

    sub = lax.broadcasted_iota(jnp.int32, (LANES, QT), 0)
    tq = qt * QT + lax.broadcasted_iota(jnp.int32, (LANES, QT), 1)
    ok_c = (sub * CMP_STRIDE + (CMP_BLOCK - 1)) <= tq
    sc = _dot(kc_ref[...].astype(BF16), qT) + tile8(jnp.where(ok_c, 0.0, NEG))
    mc = jnp.max(sc, axis=0, keepdims=True)
    ec = jnp.exp2(sc - mc) * tile8(jnp.where(ok_c, 1.0, 0.0))
    lc = jnp.sum(ec, axis=0, keepdims=True)
    pc = ec / jnp.where(lc > 0.0, lc, 1.0)
    o_c = _dot(vc_ref[...].T.astype(BF16), pc.astype(BF16))

    n_blk = 32
    ji = lax.broadcasted_iota(jnp.int32, (n_blk, QT), 0)
    jf = ji.astype(F32)
    cur = jnp.right_shift(qt * QT + lax.broadcasted_iota(jnp.int32, (n_blk, QT), 1), 6)
    allowed = ji <= cur
    forced = (ji == 0) | (ji == cur) | (ji == cur - 1)
    sel_bf = []
    for gi in range(2):
        pg = pc[:, (gi * 4) * QT:(gi * 4 + 1) * QT]
        for z in range(1, HEADS_PER_KV):
            pg = pg + pc[:, (gi * 4 + z) * QT:(gi * 4 + z + 1) * QT]
        hi = pg.astype(BF16)
        lo = (pg - hi.astype(F32)).astype(BF16)
        imp = _dot(ovt_ref[...], hi) + _dot(ovt_ref[...], lo)
        score = jnp.where(forced, FORCE, jnp.where(allowed, imp[0:n_blk], NEG))
        sel = jnp.zeros((n_blk, QT), F32)
        for _ in range(TOP_N):
            mx = jnp.max(score, axis=0, keepdims=True)
            first = jnp.min(jnp.where(score == mx, jf, 1e9), axis=0, keepdims=True)
            hit = jf == first
            sel = jnp.where(hit & (mx > NEG / 2), 1.0, sel)
            score = jnp.where(hit, -3e38, score)
        sel_bf.append(jnp.concatenate([sel, jnp.zeros((LANES - n_blk, QT), F32)], axis=0).astype(BF16))

    def slc_body(kt, carry):
        m_prev, l_prev = carry
        k0 = pl.multiple_of(kt * TK_SLC, TK_SLC)
        kpos = k0 + lax.broadcasted_iota(jnp.int32, (TK_SLC, QT), 0)
        causal = kpos <= qt * QT + lax.broadcasted_iota(jnp.int32, (TK_SLC, QT), 1)
        ex = ext_ref[pl.ds(k0, TK_SLC), :]
        parts = []
        for gi in range(2):
            mk = _dot(ex, sel_bf[gi])
            parts += [jnp.where((mk > 0.5) & causal, 0.0, NEG)] * HEADS_PER_KV
        s = _dot(ks_ref[pl.ds(k0, TK_SLC), :].astype(BF16), qT) + jnp.concatenate(parts, axis=1)
        m_new = jnp.maximum(m_prev, jnp.max(s, axis=0, keepdims=True))
        alpha = jnp.exp2(m_prev - m_new)
        p = jnp.exp2(s - m_new)
        l_new = alpha * l_prev + jnp.sum(p, axis=0, keepdims=True)
        vt = _tile_t(vs_ref, k0, TK_SLC // LANES).astype(BF16)
        acc_ref[...] = alpha * acc_ref[...] + _dot(vt, p.astype(BF16))
        return m_new, l_new

    acc_ref[...] = jnp.zeros(acc_ref.shape, F32)
    n_slc_tiles = (qt * QT + QT + TK_SLC - 1) // TK_SLC
    _, l_s = lax.fori_loop(0, n_slc_tiles, slc_body,
                           (jnp.full((1, COLS), NEG, F32), jnp.zeros((1, COLS), F32)))
    o_s = acc_ref[...] / l_s

    w0 = pl.multiple_of(jnp.maximum(qt - WINDOW // QT, 0) * QT, QT)
    d = (qt * QT + lax.broadcasted_iota(jnp.int32, (WIN_KEYS, QT), 1)
         - (w0 + lax.broadcasted_iota(jnp.int32, (WIN_KEYS, QT), 0)))
    sw = (_dot(kw_ref[pl.ds(w0, WIN_KEYS), :].astype(BF16), qT)
          + tile8(jnp.where((d >= 0) & (d < WINDOW), 0.0, NEG)))
    pw = jnp.exp2(sw - jnp.max(sw, axis=0, keepdims=True))
    vwt = _tile_t(vw_ref, w0, WIN_KEYS // LANES).astype(BF16)
    o_w = _dot(vwt, pw.astype(BF16)) / jnp.sum(pw, axis=0, keepdims=True)

    gs_ref[...] = jax.nn.sigmoid(gt_ref[...] + bg_ref[...]).T

    def gate_row(c):
        g8 = gs_ref[pl.ds(pl.multiple_of(c * N_HEADS + gp * 8, 8), 8), :]
        return jnp.concatenate([g8[hh:hh + 1, :] for hh in range(8)], axis=1)

    comb = gate_row(0) * o_c + gate_row(1) * o_s + gate_row(2) * o_w
    for m in range(4):
        r0 = (m // 2) * HEAD_DIM
        top = comb[r0:r0 + HEAD_DIM, (2 * m) * QT:(2 * m + 1) * QT]
        bot = comb[r0:r0 + HEAD_DIM, (2 * m + 1) * QT:(2 * m + 2) * QT]
        o_ref[:, m * LANES:(m + 1) * LANES] = jnp.concatenate([top, bot], axis=0).T


def _attn_consts(seq):
    n = np.arange(LANES)
    ci = n[:, None] * CMP_STRIDE
    sj = n[None, :] * SLC_BLOCK
    n_cmp = (seq - CMP_BLOCK) // CMP_STRIDE + 1
    n_slc = -(-seq // SLC_BLOCK)
    ov = ((ci < sj + SLC_BLOCK) & (ci + CMP_BLOCK > sj) & (n[:, None] < n_cmp) & (n[None, :] < n_slc))
    ex = (np.arange(seq)[None, :] // SLC_BLOCK) == n[:, None]
    return (jnp.asarray(ov.T.astype(np.float32), BF16), jnp.asarray(ex.T.astype(np.float32), BF16))


def _attn_prompt(q, kv4, kvw, kc, vc, gates, bgate, n_batch, seq):
    ovt, ext = _attn_consts(seq)
    nqt = seq // QT
    kspec = lambda col0: pl.BlockSpec((seq, LANES), lambda b, gp, t: (b, col0 + gp))
    cspec = pl.BlockSpec((None, None, 128, LANES), lambda b, gp, t: (b, gp, 0, 0))
    return pl.pallas_call(
        _attn_kernel,
        grid=(n_batch, 2, nqt),
        in_specs=[
            pl.BlockSpec((QT, 512), lambda b, gp, t: (b * nqt + t, gp)),
            kspec(4), kspec(6), kspec(0), kspec(2), cspec, cspec,
            pl.BlockSpec((QT, LANES), lambda b, gp, t: (b * nqt + t, 0)),
            pl.BlockSpec((1, LANES), lambda b, gp, t: (0, 0)),
            pl.BlockSpec(ovt.shape, lambda b, gp, t: (0, 0)),
            pl.BlockSpec(ext.shape, lambda b, gp, t: (0, 0)),
        ],
        out_specs=pl.BlockSpec((QT, 512), lambda b, gp, t: (b * nqt + t, gp)),
        out_shape=jax.ShapeDtypeStruct((n_batch * seq, 1024), F32),
        scratch_shapes=[pltpu.VMEM((LANES, COLS), F32), pltpu.VMEM((LANES, QT), F32)],
        compiler_params=_cparams(3),
        name="nsa_attn_prompt",
    )(q, kv4, kv4, kvw, kvw, kc, vc, gates, bgate, ovt, ext)


N_PAGES = 16
BPS = 2
SLAB_PITCH = 24


def _s1_kernel(layer, pt_ref, cache_ref, q_ref, wcat_ref, p_ref, w2_ref, c_ref, s1_ref, s2_ref,
               ov_ref, oc_ref, sel_ref, raw_ref, slab_ref, lhs_ref, sem_ref):
    step = pl.program_id(0)
    n_steps = pl.num_programs(0)

    def page_copy(st, slot, bi, p):
        return pltpu.make_async_copy(
            cache_ref.at[layer, pt_ref[st * BPS + bi, p], pl.ds(0, 2)],
            raw_ref.at[slot, bi, p],
            sem_ref.at[slot])

    def start_all(st, slot):
        for bi in range(BPS):
            for p in range(N_PAGES):
                page_copy(st, slot, bi, p).start()

    slot = step % 2

    @pl.when(step == 0)
    def _():
        start_all(0, 0)

    @pl.when(step + 1 < n_steps)
    def _():
        start_all(step + 1, 1 - slot)

    for bi in range(BPS):
        for p in range(N_PAGES):
            page_copy(step, slot, bi, p).wait()

    for bi in range(BPS):
        _s1_one(bi, slot, q_ref, wcat_ref, p_ref, w2_ref, c_ref, s1_ref, s2_ref, ov_ref, oc_ref, sel_ref,
                raw_ref, slab_ref, lhs_ref)


def _s1_one(bi, slot, q_ref, wcat_ref, p_ref, w2_ref, c_ref, s1_ref, s2_ref, ov_ref, oc_ref, sel_ref,
            raw_ref, slab_ref, lhs_ref):
    chunks = PAGE_SIZE // CMP_STRIDE
    for p in range(N_PAGES):
        for cb in range(4):
            kind, pr = cb // 2, cb % 2
            t = raw_ref[slot, bi, p, kind, pr * LANES:(pr + 1) * LANES, :].T
            for m in range(chunks):
                r0 = (p * chunks + m) * SLAB_PITCH
                slab_ref[bi, cb, r0:r0 + CMP_STRIDE, :] = t[m * CMP_STRIDE:(m + 1) * CMP_STRIDE]

    def load(l, cb):
        return slab_ref[bi, cb, pl.ds(l, 128, stride=SLAB_PITCH), :]

    outs = []
    for kind in range(2):
        _fill_lhs_strided(load, lhs_ref.at[bi], kind)
        outs.append(_compress_mlp(lhs_ref.at[bi], wcat_ref, p_ref, w2_ref, kind))
    ko, vo = outs
    kc = jnp.concatenate([_rope_slab(k, c_ref[...], s1_ref[...], s2_ref[...]) for k in ko], axis=1)
    vc = jnp.concatenate(vo, axis=1)

    qp = (q_ref[bi] * SCALE).astype(BF16)
    sc = _dot_nt(qp, kc.astype(BF16))
    lane = lax.broadcasted_iota(jnp.int32, (N_HEADS, LANES), 1)
    ok = lane < (LANES - 1)
    sc = jnp.where(ok, sc, NEG)
    mc = jnp.max(sc, axis=1, keepdims=True)
    ec = jnp.where(ok, jnp.exp(sc - mc), 0.0)
    pc = ec / jnp.sum(ec, axis=1, keepdims=True)
    oc_ref[bi] = _dot(pc.astype(BF16), vc.astype(BF16))

    imp_h = _split_dot(pc, ov_ref[...])
    rowg = jnp.right_shift(lax.broadcasted_iota(jnp.int32, (N_HEADS, LANES), 0), 2)
    row8 = lax.broadcasted_iota(jnp.int32, (8, LANES), 0)
    imp = jnp.zeros((8, LANES), F32)
    for g in range(N_KV):
        ig = jnp.sum(jnp.where(rowg == g, imp_h, 0.0), axis=0, keepdims=True)
        imp = jnp.where(row8 == g, ig, imp)
    lane8 = lax.broadcasted_iota(jnp.int32, (8, LANES), 1)
    jf = lane8.astype(F32)
    last = 2048 // SLC_BLOCK
    score = jnp.where((lane8 >= 1) & (lane8 <= last - 2), imp, NEG)
    picks = jnp.where(lane8 == 6, float(last - 1), 0.0)
    for k in range(TOP_N - 3):
        mx = jnp.max(score, axis=1, keepdims=True)
        first = jnp.min(jnp.where(score == mx, jf, 1e9), axis=1, keepdims=True)
        picks = jnp.where(lane8 == k, first, picks)
        score = jnp.where(jf == first, -3e38, score)
    sel_ref[bi] = picks.astype(jnp.int32)


def _s1(layer, page_table, cache, q_exp, wcat, pflat, w2bd, ctabs, ov):
    nb = q_exp.shape[0]
    full = lambda a: pl.BlockSpec(a.shape, lambda b, pt: (0,) * a.ndim)
    grid_spec = pltpu.PrefetchScalarGridSpec(
        num_scalar_prefetch=1,
        grid=(nb // BPS,),
        in_specs=[pl.BlockSpec(memory_space=pl.ANY),
                  pl.BlockSpec((BPS, N_HEADS, 256), lambda b, pt: (b, 0, 0)),
                  full(wcat), full(pflat), full(w2bd), full(ctabs[0]), full(ctabs[1]), full(ctabs[2]),
                  full(ov)],
        out_specs=[pl.BlockSpec((BPS, N_HEADS, 256), lambda b, pt: (b, 0, 0)),
                   pl.BlockSpec((BPS, 8, LANES), lambda b, pt: (b, 0, 0))],
        scratch_shapes=[pltpu.VMEM((2, BPS, N_PAGES, 2, 256, PAGE_SIZE), F32),
                        pltpu.VMEM((BPS, 4, 128 * SLAB_PITCH, LANES), F32),
                        pltpu.VMEM((BPS, 512, 1024), BF16),
                        pltpu.SemaphoreType.DMA((2,))],
    )
    return pl.pallas_call(
        functools.partial(_s1_kernel, layer),
        grid_spec=grid_spec,
        out_shape=[jax.ShapeDtypeStruct((nb, N_HEADS, 256), F32),
                   jax.ShapeDtypeStruct((nb, 8, LANES), jnp.int32)],
        compiler_params=_cparams(1),
        name="nsa_sample_s1",
    )(page_table, cache, q_exp, wcat, pflat, w2bd, *ctabs, ov)


N_HIST = TOP_N - 1
KSEL = N_HIST * PAGE_SIZE


def _s2_kernel(layer, has_prev, pt_ref, sel_ref, cache_ref, win_ref, q_ref, oc_ref, kvn_ref, kwn_ref,
               gt_ref, bg_ref, hsel_ref, *rest):
    o_ref, nwin_ref, kbuf_ref, vbuf_ref, sem_ref = rest[1:] if has_prev else rest
    if has_prev:
        nwin_ref[0:layer] = rest[0][...]
    step = pl.program_id(0)
    n_steps = pl.num_programs(0)

    def copies(st, slot, bi, g, k):
        bb = st * BPS + bi
        blk = sel_ref[(bb * N_KV + g) * 8 + k]
        page = pt_ref[bb, jnp.right_shift(blk, 1)]
        rows = pl.ds(g * HEAD_DIM, HEAD_DIM)
        dst = pl.ds(k * PAGE_SIZE, PAGE_SIZE)
        return (pltpu.make_async_copy(cache_ref.at[layer, page, 2, rows, :],
                                      kbuf_ref.at[slot, bi, g, :, dst], sem_ref.at[slot]),
                pltpu.make_async_copy(cache_ref.at[layer, page, 3, rows, :],
                                      vbuf_ref.at[slot, bi, g, :, dst], sem_ref.at[slot]))

    def for_all(st, slot, fn):
        for bi in range(BPS):
            for g in range(N_KV):
                for k in range(N_HIST):
                    ck, cv = copies(st, slot, bi, g, k)
                    fn(ck)
                    fn(cv)

    slot = step % 2

    @pl.when(step == 0)
    def _():
        for_all(0, 0, lambda c: c.start())

    @pl.when(step + 1 < n_steps)
    def _():
        for_all(step + 1, 1 - slot, lambda c: c.start())

    for_all(step, slot, lambda c: c.wait())

    for bi in range(BPS):
        _s2_one(layer, bi, step * BPS + bi, slot, sel_ref, win_ref, q_ref, oc_ref, kvn_ref, kwn_ref,
                gt_ref, bg_ref, hsel_ref, o_ref, nwin_ref, kbuf_ref, vbuf_ref)


def _s2_one(layer, bi, b, slot, sel_ref, win_ref, q_ref, oc_ref, kvn_ref, kwn_ref, gt_ref, bg_ref,
            hsel_ref, o_ref, nwin_ref, kbuf_ref, vbuf_ref):
    qf = q_ref[bi] * SCALE
    qp = qf.astype(BF16)
    q16 = (qf[:, 0:64] + qf[:, 64:128] + qf[:, 128:192] + qf[:, 192:256]).astype(BF16)
    lane_k = lax.broadcasted_iota(jnp.int32, (N_HEADS, KSEL), 1)
    rowg_k = jnp.right_shift(lax.broadcasted_iota(jnp.int32, (N_HEADS, KSEL), 0), 2)
    rowg_d = jnp.right_shift(lax.broadcasted_iota(jnp.int32, (N_HEADS, HEAD_DIM), 0), 2)
    tile_k = jnp.right_shift(lane_k, 7)
    half_k = jnp.bitwise_and(jnp.right_shift(lane_k, 6), 1)

    def attend(s_hist, s_new, pv_hist, v_new):
        m = jnp.maximum(jnp.max(s_hist, axis=1, keepdims=True), s_new)
        e = jnp.exp(s_hist - m)
        en = jnp.exp(s_new - m)
        l = jnp.sum(e, axis=1, keepdims=True) + en
        return (pv_hist(e.astype(BF16)) + en * v_new) / l

    kn = kvn_ref[bi]
    s_hist = jnp.zeros((N_HEADS, KSEL), F32)
    v16 = jnp.zeros((N_HEADS, HEAD_DIM), F32)
    for g in range(N_KV):
        sg = _dot(q16, kbuf_ref[slot, bi, g].astype(BF16))
        want = jnp.zeros((N_HEADS, KSEL), jnp.int32)
        for k in range(N_HIST):
            half = jnp.bitwise_and(sel_ref[(b * N_KV + g) * 8 + k], 1)
            want = jnp.where(tile_k == k, half, want)
        s_hist = jnp.where(rowg_k == g, jnp.where(half_k == want, sg, NEG), s_hist)
        v16 = jnp.where(rowg_d == g, kn[:, 768 + g * HEAD_DIM:768 + (g + 1) * HEAD_DIM], v16)
    s_new = jnp.sum(qf * kn[:, 512:768], axis=1, keepdims=True)

    def pv_slc(e):
        o = jnp.zeros((N_HEADS, HEAD_DIM), F32)
        for g in range(N_KV):
            o = jnp.where(rowg_d == g, _dot_nt(e, vbuf_ref[slot, bi, g].astype(BF16)), o)
        return o

    o_s = attend(s_hist, s_new, pv_slc, v16)
    o_s = jnp.concatenate([o_s] * N_KV, axis=1)

    wn = kwn_ref[bi]
    s_w = _dot(qp, win_ref[bi, 0].astype(BF16))
    colw = lax.broadcasted_iota(jnp.int32, (N_HEADS, WINDOW), 1)
    s_w = jnp.where(colw >= 1, s_w, NEG)
    s_wn = jnp.sum(qf * wn[:, 0:256], axis=1, keepdims=True)
    vw = win_ref[bi, 1].astype(BF16)
    o_w = attend(s_w, s_wn, lambda e: _dot_nt(e, vw), wn[:, 256:512])

    lane = lax.broadcasted_iota(jnp.int32, (LANES, LANES), 1)
    for c in range(2):
        for rs in range(2):
            rows = slice(rs * LANES, (rs + 1) * LANES)
            new = wn[:, c * 256 + rs * LANES:c * 256 + (rs + 1) * LANES]
            col = jnp.broadcast_to(new, (LANES, LANES)).T
            tiles = [pltpu.roll(win_ref[bi, c, rows, j * LANES:(j + 1) * LANES], LANES - 1, 1)
                     for j in range(WINDOW // LANES)]
            tiles.append(col)
            for j in range(WINDOW // LANES):
                nwin_ref[layer, bi, c, rows, j * LANES:(j + 1) * LANES] = jnp.where(
                    lane == LANES - 1, tiles[j + 1], tiles[j])

    gsig = jax.nn.sigmoid(gt_ref[bi] + bg_ref[...])
    gcol = lambda c: jnp.sum(hsel_ref[c] * gsig, axis=1, keepdims=True)
    o_ref[bi] = gcol(0) * oc_ref[bi] + gcol(1) * o_s + gcol(2) * o_w


def _s2(layer, page_table, sel_flat, cache, win, q_exp, oc, kv4s, kvws, gates, bgate, hsel, prev):
    nb = q_exp.shape[0]
    hq = pl.BlockSpec((BPS, N_HEADS, 256), lambda b, pt, sl: (b, 0, 0))
    row = lambda w: pl.BlockSpec((BPS, 1, w), lambda b, pt, sl: (b, 0, 0))
    wspec = pl.BlockSpec((None, BPS, 2, 256, WINDOW), lambda b, pt, sl: (layer, b, 0, 0, 0))
    stack = lambda n: pl.BlockSpec((n, BPS, 2, 256, WINDOW), lambda b, pt, sl: (0, b, 0, 0, 0))
    in_specs = [pl.BlockSpec(memory_space=pl.ANY), wspec,
                hq, hq, row(1024), row(512), row(LANES),
                pl.BlockSpec((1, LANES), lambda b, pt, sl: (0, 0)),
                pl.BlockSpec(hsel.shape, lambda b, pt, sl: (0, 0, 0))]
    args = [page_table, sel_flat, cache, win, q_exp, oc, kv4s, kvws, gates, bgate, hsel]
    if prev is not None:
        in_specs.append(stack(layer))
        args.append(prev)
    grid_spec = pltpu.PrefetchScalarGridSpec(
        num_scalar_prefetch=2,
        grid=(nb // BPS,),
        in_specs=in_specs,
        out_specs=[hq, stack(layer + 1)],
        scratch_shapes=[pltpu.VMEM((2, BPS, N_KV, HEAD_DIM, KSEL), F32),
                        pltpu.VMEM((2, BPS, N_KV, HEAD_DIM, KSEL), F32),
                        pltpu.SemaphoreType.DMA((2,))],
    )
    return pl.pallas_call(
        functools.partial(_s2_kernel, layer, prev is not None),
        grid_spec=grid_spec,
        out_shape=[jax.ShapeDtypeStruct((nb, N_HEADS, 256), F32),
                   jax.ShapeDtypeStruct((layer + 1, nb, 2, 256, WINDOW), F32)],
        compiler_params=_cparams(1),
        name="nsa_sample_s2",
    )(*args)


def _sg_in_kernel(x_ref, g_ref, w_ref, lg_ref, lb_ref, u_ref, v_ref):
    xb = _rmsnorm(x_ref[...], g_ref[...]).astype(BF16)
    u_ref[...] = jax.nn.gelu(_dot(xb, w_ref[:, 0:1024]))
    v = jax.nn.gelu(_dot(xb, w_ref[:, 1024:2048]))
    mu = jnp.mean(v, axis=-1, keepdims=True)
    var = jnp.mean(jnp.square(v - mu), axis=-1, keepdims=True)
    v_ref[...] = (v - mu) * lax.rsqrt(var + EPS) * lg_ref[...] + lb_ref[...]


def _sg_in(x, g, w, lg, lb, tm):
    m = x.shape[0]
    full = lambda a: pl.BlockSpec(a.shape, lambda i: (0,) * a.ndim, pipeline_mode=pl.Buffered(1))
    row = pl.BlockSpec((tm, 1024), lambda i: (i, 0))
    return pl.pallas_call(
        _sg_in_kernel,
        grid=(m // tm,),
        in_specs=[row, full(g), full(w), full(lg), full(lb)],
        out_specs=[row, row],
        out_shape=[jax.ShapeDtypeStruct((m, 1024), F32)] * 2,
        compiler_params=_cparams(1),
        name="sg_in",
    )(x, g, w, lg, lb)


FF_CHUNK = 1024


def _ffn_tail(final, x1, g_ref, w1_ref, w2_ref, gfin_ref, o_ref):
    xb = _rmsnorm(x1, g_ref[...]).astype(BF16)
    acc = x1
    for c in range(D_FF // FF_CHUNK):
        sl = slice(c * FF_CHUNK, (c + 1) * FF_CHUNK)
        h = jnp.maximum(_dot(xb, w1_ref[:, sl]), 0.0)
        acc = acc + _dot((h * h).astype(BF16), w2_ref[sl, :])
    o_ref[...] = _rmsnorm(acc, gfin_ref[...]) if final else acc


def _post_kernel(final, x_ref, a_ref, wo_ref, g_ref, w1_ref, w2_ref, gfin_ref, o_ref):
    x1 = x_ref[...] + _dot(a_ref[...].astype(BF16), wo_ref[...])
    _ffn_tail(final, x1, g_ref, w1_ref, w2_ref, gfin_ref, o_ref)


def _post_sg_prompt_kernel(final, x_ref, u_ref, v_ref, ws_ref, bs_ref, wo_ref, g_ref, w1_ref, w2_ref,
                           gfin_ref, o_ref, a_ref):
    r = lax.broadcasted_iota(jnp.int32, (CHUNK, CHUNK), 0)
    c = lax.broadcasted_iota(jnp.int32, (CHUNK, CHUNK), 1)
    tril = r >= c
    for g in range(SG_GROUPS):
        w = jnp.where(tril, ws_ref[g], 0.0).astype(BF16)
        gl = slice(g * 128, (g + 1) * 128)
        for ch in range(x_ref.shape[0] // CHUNK):
            rs = slice(ch * CHUNK, (ch + 1) * CHUNK)
            s = _dot(w, v_ref[rs, gl].astype(BF16)) + bs_ref[:, gl]
            a_ref[rs, gl] = (u_ref[rs, gl] * s).astype(BF16)
    x1 = x_ref[...] + _dot(a_ref[...], wo_ref[...])
    _ffn_tail(final, x1, g_ref, w1_ref, w2_ref, gfin_ref, o_ref)


def _post_sg_sample_kernel(final, x_ref, u_ref, v_ref, w0_ref, b0_ref, wo_ref, g_ref, w1_ref, w2_ref,
                           gfin_ref, o_ref):
    a = u_ref[...] * (v_ref[...] * w0_ref[...] + b0_ref[...])
    x1 = x_ref[...] + _dot(a.astype(BF16), wo_ref[...])
    _ffn_tail(final, x1, g_ref, w1_ref, w2_ref, gfin_ref, o_ref)


def _post_call(kern, name, final, x, row_ins, full_ins, tm, scratch=()):
    m = x.shape[0]
    full = lambda a: pl.BlockSpec(a.shape, lambda i: (0,) * a.ndim, pipeline_mode=pl.Buffered(1))
    row = lambda a: pl.BlockSpec((tm, a.shape[1]), lambda i: (i, 0))
    return pl.pallas_call(
        functools.partial(kern, final),
        grid=(m // tm,),
        in_specs=[row(a) for a in row_ins] + [full(a) for a in full_ins],
        out_specs=pl.BlockSpec((tm, D_MODEL), lambda i: (i, 0)),
        out_shape=jax.ShapeDtypeStruct((m, D_MODEL), F32),
        scratch_shapes=list(scratch),
        compiler_params=_cparams(1),
        name=name,
    )(*row_ins, *full_ins)


def _expand_heads(w, axis):
    parts = []
    for h in range(N_HEADS):
        s = h // HEADS_PER_KV
        pad = [(0, 0), (0, 0)]
        pad[axis] = (s * HEAD_DIM, (N_KV - 1 - s) * HEAD_DIM)
        parts.append(jnp.pad(lax.slice_in_dim(w, h * HEAD_DIM, (h + 1) * HEAD_DIM, axis=axis), pad))
    return jnp.concatenate(parts, axis=axis)


def kernel(x_prompt, x_sample, cache_nsa_kv, state_nsa_win, page_table, g_mix, g_ffn, g_final,
           nsa_w_in, nsa_b_gate, nsa_cmp_pos, nsa_cmp_w1, nsa_cmp_w2, nsa_w_out,
           sg_w_in, sg_ln_g, sg_ln_b, sg_w_spatial, sg_b_spatial, sg_w_out, ffn_w1, ffn_w2):
    nb, seq, _ = x_prompt.shape
    nd = x_sample.shape[0]
    depth = g_mix.shape[0]
    n_pool = cache_nsa_kv.shape[1]
    past = page_table.shape[1] * PAGE_SIZE
    tm_p = 512

    xp = x_prompt.reshape(nb * seq, D_MODEL)
    xs = x_sample.reshape(nd, D_MODEL)
    cache = cache_nsa_kv.transpose(0, 1, 3, 4, 5, 2).reshape(cache_nsa_kv.shape[0], n_pool, 4, 256,
                                                              PAGE_SIZE)
    win = state_nsa_win.transpose(0, 1, 3, 4, 5, 2).reshape(state_nsa_win.shape[0], nd, 2, 256, WINDOW)
    nwin = None

    tabs_p = _rope_tables(jnp.arange(seq, dtype=jnp.int32))
    tabs_s = _rope_tables(jnp.full((nd,), past, jnp.int32))
    tabs_c = _rope_tables(jnp.arange(LANES, dtype=jnp.int32) * CMP_STRIDE + CMP_BLOCK - 1)

    nidx = np.arange(LANES)
    ci = nidx[:, None] * CMP_STRIDE
    sj = nidx[None, :] * SLC_BLOCK
    ov_s = ((ci < sj + SLC_BLOCK) & (ci + CMP_BLOCK > sj) & (nidx[:, None] < LANES - 1))
    ov_s = jnp.asarray(ov_s.astype(np.float32), BF16)
    hsel = np.zeros((3, N_HEADS, LANES), np.float32)
    for c in range(3):
        hsel[c, np.arange(N_HEADS), c * N_HEADS + np.arange(N_HEADS)] = 1.0
    hsel = jnp.asarray(hsel)

    gfin = g_final.reshape(1, D_MODEL)
    kvt, kwt = None, None
    kv_s, v_s = [], []
    for i in range(depth):
        j = i // 2
        final = i == depth - 1
        gm = g_mix[i].reshape(1, D_MODEL)
        gf = g_ffn[i].reshape(1, D_MODEL)
        w1 = ffn_w1[i].astype(BF16)
        w2 = ffn_w2[i].astype(BF16)
        if i % 2 == 0:
            w_in = nsa_w_in[j]
            wq = w_in[:, 0:1024].astype(BF16)
            wq_exp = _expand_heads(wq, 1)
            wkv = w_in[:, 1024:2048].astype(BF16)
            wkw = w_in[:, 2048:2560].astype(BF16)
            wg = jnp.pad(w_in[:, 2560:2608], ((0, 0), (0, LANES - 48))).astype(BF16)
            bgate = jnp.pad(nsa_b_gate[j], (0, LANES - 48)).reshape(1, LANES)
            w1c = nsa_cmp_w1[j]
            wcat = jnp.concatenate([w1c[:, :16].reshape(2, 1024, CMP_HID),
                                    w1c[:, 16:].reshape(2, 1024, CMP_HID)], axis=2).astype(BF16)
            pe = nsa_cmp_pos[j]
            pflat = jnp.concatenate([pe[:, :16].reshape(2, 1, 1024), pe[:, 16:].reshape(2, 1, 1024),
                                     jnp.zeros((2, 6, 1024), F32)], axis=1).astype(BF16)
            w2c = nsa_cmp_w2[j]
            zc = jnp.zeros_like(w2c)
            w2bd = jnp.concatenate([jnp.concatenate([w2c, zc], axis=2),
                                    jnp.concatenate([zc, w2c], axis=2)], axis=1).astype(BF16)
            wo = nsa_w_out[j].astype(BF16)
            wo_exp = _expand_heads(wo, 0)

            q_p, kv4_p, kvw_p, gt_p, kvt, kwt = _proj(xp, gm, tabs_p, seq // tm_p, wq, wkv, wkw, wg, tm_p,
                                                      stack=(j, nb, seq, kvt, kwt))
            q_s, kv4_s, kvw_s, gt_s = _proj(xs, gm, tabs_s, 1, wq_exp, wkv, wkw, wg, nd)

            kc, vc = _cmp_prompt(kv4_p, nb, seq, wcat, pflat, w2bd, tabs_c)
            a_p = _attn_prompt(q_p, kv4_p, kvw_p, kc, vc, gt_p, bgate, nb, seq)

            q_exp = q_s.reshape(nd, N_HEADS, 256)
            oc_s, sel = _s1(j, page_table, cache, q_exp, wcat, pflat, w2bd, tabs_c, ov_s)
            sel_flat = sel[:, :N_KV, :8].reshape(-1)
            a_s, nwin = _s2(j, page_table, sel_flat, cache, win, q_exp, oc_s,
                            kv4_s.reshape(nd, 1, 1024), kvw_s.reshape(nd, 1, 512),
                            gt_s.reshape(nd, 1, LANES), bgate, hsel, nwin)

            xp = _post_call(_post_kernel, "post_nsa", final, xp, [xp, a_p], [wo, gf, w1, w2, gfin], tm_p)
            xs = _post_call(_post_kernel, "post_nsa", final, xs, [xs, a_s.reshape(nd, 4096)],
                            [wo_exp, gf, w1, w2, gfin], nd)
            kv_s.append(kv4_s.reshape(nd, 1, 4, N_KV, HEAD_DIM))
        else:
            w_in = sg_w_in[j].astype(BF16)
            lg = sg_ln_g[j].reshape(1, 1024)
            lb = sg_ln_b[j].reshape(1, 1024)
            ws = sg_w_spatial[j]
            bs = sg_b_spatial[j]
            bs_exp = jnp.repeat(bs.T, CHUNK, axis=1)
            w0 = jnp.repeat(ws[:, 0, 0], CHUNK).reshape(1, 1024)
            b0 = bs_exp[0:1]
            wo = sg_w_out[j].astype(BF16)

            u_p, v_p = _sg_in(xp, gm, w_in, lg, lb, tm_p)
            u_s, vv_s = _sg_in(xs, gm, w_in, lg, lb, nd)
            xp = _post_call(_post_sg_prompt_kernel, "post_sg_prompt", final, xp, [xp, u_p, v_p],
                            [ws, bs_exp, wo, gf, w1, w2, gfin], tm_p,
                            scratch=[pltpu.VMEM((tm_p, 1024), BF16)])
            xs = _post_call(_post_sg_sample_kernel, "post_sg_sample", final, xs, [xs, u_s, vv_s],
                            [w0, b0, wo, gf, w1, w2, gfin], nd)
            v_s.append(vv_s.reshape(nd, 1, 1024))

    y_prompt = xp.reshape(nb, seq, D_MODEL)
    y_sample = xs.reshape(nd, 1, D_MODEL)
    n_l = kvt.shape[0]
    to_rows = lambda a, kinds, rows: a.reshape(n_l, a.shape[1], kinds, N_KV, HEAD_DIM, rows).transpose(
        0, 1, 5, 2, 3, 4)
    return (y_prompt, y_sample, to_rows(kvt, 4, seq), to_rows(kwt, 2, WINDOW), jnp.stack(kv_s),
            to_rows(nwin, 2, WINDOW), jnp.stack(v_s))
```

```python
import functools

import numpy as np
import jax
import jax.numpy as jnp
from jax import lax
from jax.experimental import pallas as pl
from jax.experimental.pallas import tpu as pltpu

F32 = jnp.float32
BF16 = jnp.bfloat16

D_MODEL = 1024
N_HEADS = 16
HEAD_DIM = 64
N_KV = 4
HEADS_PER_KV = 4
ROPE_DIM = 16
ROPE_THETA = 500000.0
CMP_BLOCK = 32
CMP_STRIDE = 16
CMP_HID = 128
SLC_BLOCK = 64
TOP_N = 8
WINDOW = 512
PAGE_SIZE = 128
CHUNK = 128
SG_GROUPS = 8
D_FF = 4096
EPS = 1e-6
NEG = -1e30
FORCE = 1e6
SCALE = HEAD_DIM ** -0.5
LOG2E = 1.4426950408889634

LANES = 128
VMEM_LIMIT = 56 * 1024 * 1024


def _cparams(n_axes):
    return pltpu.CompilerParams(dimension_semantics=("arbitrary",) * n_axes,
                                vmem_limit_bytes=VMEM_LIMIT)


def _dot(a, b):
    return jnp.dot(a, b, preferred_element_type=F32)


def _dot_nt(a, b):
    return lax.dot_general(a, b, (((1,), (1,)), ((), ())), preferred_element_type=F32)


def _split_dot(a, b):
    hi = a.astype(BF16)
    lo = (a - hi.astype(F32)).astype(BF16)
    return _dot(hi, b) + _dot(lo, b)


def _rmsnorm(x, g):
    return x * lax.rsqrt(jnp.mean(x * x, axis=-1, keepdims=True) + EPS) * g


def _rope_slab(x, c, s1, s2):
    return x * c + pltpu.roll(x, LANES - 8, 1) * s1 + pltpu.roll(x, 8, 1) * s2


def _rope_tables(pos):
    half = ROPE_DIM // 2
    inv = jnp.power(jnp.float32(ROPE_THETA), -jnp.arange(half, dtype=F32) / half)
    ang = pos.astype(F32)[:, None] * inv[None, :]
    cos, sin = jnp.cos(ang), jnp.sin(ang)
    n = pos.shape[0]
    z = lambda w: jnp.zeros((n, w), F32)
    c = jnp.concatenate([cos, cos, jnp.ones((n, HEAD_DIM - ROPE_DIM), F32)], axis=1)
    s1 = jnp.concatenate([-sin, z(HEAD_DIM - half)], axis=1)
    s2 = jnp.concatenate([z(half), sin, z(HEAD_DIM - ROPE_DIM)], axis=1)
    t2 = lambda a: jnp.concatenate([a, a], axis=1)
    return t2(c), t2(s1), t2(s2)


def _proj_kernel(stack_t, layer, tiles_per_seq, x_ref, g_ref, c_ref, s1_ref, s2_ref, wq_ref, wkv_ref,
                 wkw_ref, wg_ref, *rest):
    if stack_t and layer > 0:
        pkv_ref, pkw_ref = rest[:2]
        rest = rest[2:]
    q_ref, kv_ref, kw_ref, gt_ref = rest[:4]
    xb = _rmsnorm(x_ref[...], g_ref[...]).astype(BF16)
    c, s1, s2 = c_ref[...], s1_ref[...], s2_ref[...]
    q = _dot(xb, wq_ref[...])
    for j in range(q.shape[1] // LANES):
        sl = slice(j * LANES, (j + 1) * LANES)
        q_ref[:, sl] = _rope_slab(q[:, sl], c, s1, s2)
    kv = _dot(xb, wkv_ref[...])
    kv_ref[:, 0:512] = kv[:, 0:512]
    for j in (4, 5):
        sl = slice(j * LANES, (j + 1) * LANES)
        kv_ref[:, sl] = _rope_slab(kv[:, sl], c, s1, s2)
    kv_ref[:, 768:1024] = kv[:, 768:1024]
    kw = _dot(xb, wkw_ref[...])
    for j in (0, 1):
        sl = slice(j * LANES, (j + 1) * LANES)
        kw_ref[:, sl] = _rope_slab(kw[:, sl], c, s1, s2)
    kw_ref[:, 256:512] = kw[:, 256:512]
    gt_ref[...] = _dot(xb, wg_ref[...])
    if not stack_t:
        return
    kvt_ref, kwt_ref = rest[4:6]
    tm = x_ref.shape[0]

    def put_t(dst_ref, src_ref, width):
        for r in range(tm // LANES):
            for cb in range(width // LANES):
                dst_ref[layer, cb * LANES:(cb + 1) * LANES, r * LANES:(r + 1) * LANES] = (
                    src_ref[r * LANES:(r + 1) * LANES, cb * LANES:(cb + 1) * LANES].T)

    if layer > 0:
        kvt_ref[0:layer] = pkv_ref[...]
    put_t(kvt_ref, kv_ref, 1024)

    @pl.when(pl.program_id(0) % tiles_per_seq == tiles_per_seq - 1)
    def _():
        if layer > 0:
            kwt_ref[0:layer] = pkw_ref[...]
        put_t(kwt_ref, kw_ref, 512)


def _proj(x, g, tabs, tab_period_blocks, wq, wkv, wkw, wg, tm, stack=None):
    m = x.shape[0]
    nq = wq.shape[1]
    full = lambda a: pl.BlockSpec(a.shape, lambda i: (0,) * a.ndim, pipeline_mode=pl.Buffered(1))
    tab_spec = pl.BlockSpec((tm, LANES), lambda i: (i % tab_period_blocks, 0))
    row = lambda w: pl.BlockSpec((tm, w), lambda i: (i, 0))
    in_specs = [row(D_MODEL), full(g), tab_spec, tab_spec, tab_spec,
                full(wq), full(wkv), full(wkw), full(wg)]
    args = [x, g, *tabs, wq, wkv, wkw, wg]
    out_specs = [row(nq), row(1024), row(512), row(LANES)]
    out_shape = [jax.ShapeDtypeStruct((m, nq), F32), jax.ShapeDtypeStruct((m, 1024), F32),
                 jax.ShapeDtypeStruct((m, 512), F32), jax.ShapeDtypeStruct((m, LANES), F32)]
    layer, tps = 0, 1
    if stack is not None:
        layer, n_batch, seq, prev_kvt, prev_kwt = stack
        assert tm == WINDOW and seq % tm == 0
        tps = seq // tm
        kvt = lambda n: pl.BlockSpec((n, None, 1024, tm), lambda i: (0, i // tps, 0, i % tps))
        kwt = lambda n: pl.BlockSpec((n, None, 512, WINDOW), lambda i: (0, i // tps, 0, 0))
        if layer > 0:
            in_specs += [kvt(layer), kwt(layer)]
            args += [prev_kvt, prev_kwt]
        out_specs += [kvt(layer + 1), kwt(layer + 1)]
        out_shape += [jax.ShapeDtypeStruct((layer + 1, n_batch, 1024, seq), F32),
                      jax.ShapeDtypeStruct((layer + 1, n_batch, 512, WINDOW), F32)]
    return pl.pallas_call(
        functools.partial(_proj_kernel, stack is not None, layer, tps),
        grid=(m // tm,),
        in_specs=in_specs,
        out_specs=out_specs,
        out_shape=out_shape,
        compiler_params=_cparams(1),
        name="nsa_proj",
    )(*args)


def _fill_lhs_strided(load, lhs_ref, kv):
    lane = lax.broadcasted_iota(jnp.int32, (128, LANES), 1)
    low = lane < 64
    for v in range(2):
        col = kv * 2 + v
        for lp in range(8):
            xe = load(2 * lp, col)
            xo = load(2 * lp + 1, col)
            re = pltpu.roll(xe, 64, 1)
            ro = pltpu.roll(xo, 64, 1)
            dst = slice(lp * LANES, (lp + 1) * LANES)
            lhs_ref[(2 * v) * 128:(2 * v + 1) * 128, dst] = jnp.where(low, xe, ro).astype(BF16)
            lhs_ref[(2 * v + 1) * 128:(2 * v + 2) * 128, dst] = jnp.where(low, re, xo).astype(BF16)


def _compress_mlp(lhs_ref, wcat_ref, p_ref, w2_ref, kv):
    w = wcat_ref[kv]
    c = _dot(lhs_ref[...], w)
    pb = _dot(p_ref[kv], w)
    bias = pb[0:1, 0:128] + pb[1:2, 128:256]
    hids = []
    for g in range(N_KV):
        lo = c[g * 128:(g + 1) * 128, 0:128]
        hi = c[g * 128:(g + 1) * 128, 128:256]
        pre = lo + pltpu.roll(hi, 127, 0) + bias
        hids.append(jax.nn.gelu(pre).astype(BF16))
    outs = []
    for pr in range(2):
        hc = jnp.concatenate([hids[2 * pr], hids[2 * pr + 1]], axis=1)
        outs.append(_dot(hc, w2_ref[kv]))
    return outs


def _cmp_prompt_kernel(x0_ref, x1_ref, x2_ref, x3_ref, wcat_ref, p_ref, w2_ref, c_ref, s1_ref, s2_ref,
                       kc_ref, vc_ref, lhs_ref):
    xs = (x0_ref, x1_ref, x2_ref, x3_ref)

    def load(l, cb):
        return xs[cb][pl.ds(l, 128, stride=CMP_STRIDE), :]
    outs = []
    for kv in range(2):
        _fill_lhs_strided(load, lhs_ref, kv)
        outs.append(_compress_mlp(lhs_ref, wcat_ref, p_ref, w2_ref, kv))
    ko, vo = outs
    for pr in range(2):
        kc_ref[pr] = _rope_slab(ko[pr], c_ref[...], s1_ref[...], s2_ref[...])
        vc_ref[pr] = vo[pr]


def _cmp_prompt(kv4, n_batch, seq, wcat, pflat, w2bd, ctabs):
    full = lambda a: pl.BlockSpec(a.shape, lambda b: (0,) * a.ndim)
    out = pl.BlockSpec((None, 2, 128, LANES), lambda b: (b, 0, 0, 0))
    return pl.pallas_call(
        _cmp_prompt_kernel,
        grid=(n_batch,),
        in_specs=[pl.BlockSpec((seq, LANES), functools.partial(lambda cb, b: (b, cb), cb))
                  for cb in range(4)] + [full(wcat), full(pflat), full(w2bd),
                  full(ctabs[0]), full(ctabs[1]), full(ctabs[2])],
        out_specs=[out, out],
        out_shape=[jax.ShapeDtypeStruct((n_batch, 2, 128, LANES), F32)] * 2,
        scratch_shapes=[pltpu.VMEM((512, 1024), BF16)],
        compiler_params=_cparams(1),
        name="nsa_cmp_prompt",
    )(kv4, kv4, kv4, kv4, wcat, pflat, w2bd, *ctabs)


QT = 256
COLS = 2 * HEADS_PER_KV * QT
TK_SLC = 512
WIN_KEYS = WINDOW + QT


def _tile_t(ref, r0, n_tiles):
    return jnp.concatenate([ref[pl.ds(r0 + i * LANES, LANES), :].T for i in range(n_tiles)], axis=1)


def _attn_kernel(q_ref, ks_ref, vs_ref, kw_ref, vw_ref, kc_ref, vc_ref, gt_ref, bg_ref,
                 ovt_ref, o_ref, acc_ref, gs_ref, sel_ref):
    gp = pl.program_id(1)
    qt = pl.program_id(2)

    zeros64 = jnp.zeros((HEAD_DIM, QT), F32)
    blocks = []
    for m in range(4):
        t = q_ref[:, m * LANES:(m + 1) * LANES].T * (SCALE * LOG2E)
        for par in range(2):
            dims = t[par * HEAD_DIM:(par + 1) * HEAD_DIM]
            blocks.append(jnp.concatenate([dims, zeros64] if m < 2 else [zeros64, dims], axis=0))
    qT = jnp.concatenate(blocks, axis=1).astype(BF16)

    def tile8(x):
        return jnp.concatenate([x] * 8, axis=1)

    sub = lax.broadcasted_iota(jnp.int32, (LANES, QT), 0)
    tq = qt * QT + lax.broadcasted_iota(jnp.int32, (LANES, QT), 1)
    ok_c = (sub * CMP_STRIDE + (CMP_BLOCK - 1)) <= tq
    sc = _dot(kc_ref[...].astype(BF16), qT) + tile8(jnp.where(ok_c, 0.0, NEG))
    mc = jnp.max(sc, axis=0, keepdims=True)
    ec = jnp.exp2(sc - mc) * tile8(jnp.where(ok_c, 1.0, 0.0))
    lc = jnp.sum(ec, axis=0, keepdims=True)
    pc = ec / jnp.where(lc > 0.0, lc, 1.0)
    o_c = _dot(vc_ref[...].T.astype(BF16), pc.astype(BF16))

    n_blk = 32
    ji = lax.broadcasted_iota(jnp.int32, (n_blk, QT), 0)
    jf = ji.astype(F32)
    cur = jnp.right_shift(qt * QT + lax.broadcasted_iota(jnp.int32, (n_blk, QT), 1), 6)
    allowed = ji <= cur
    forced = (ji == 0) | (ji == cur) | (ji == cur - 1)
    for gi in range(2):
        pg = pc[:, (gi * 4) * QT:(gi * 4 + 1) * QT]
        for z in range(1, HEADS_PER_KV):
            pg = pg + pc[:, (gi * 4 + z) * QT:(gi * 4 + z + 1) * QT]
        hi = pg.astype(BF16)
        lo = (pg - hi.astype(F32)).astype(BF16)
        imp = _dot(ovt_ref[...], hi) + _dot(ovt_ref[...], lo)
        score = jnp.where(forced, FORCE, jnp.where(allowed, imp[0:n_blk], NEG))
        sel = jnp.zeros((n_blk, QT), F32)
        for _ in range(TOP_N):
            mx = jnp.max(score, axis=0, keepdims=True)
            first = jnp.min(jnp.where(score == mx, jf, 1e9), axis=0, keepdims=True)
            hit = jf == first
            sel = jnp.where(hit & (mx > NEG / 2), 1.0, sel)
            score = jnp.where(hit, -3e38, score)
        sel_ref[gi] = sel

    def slc_body(kt, carry):
        m_prev, l_prev = carry
        k0 = pl.multiple_of(kt * TK_SLC, TK_SLC)
        kpos = k0 + lax.broadcasted_iota(jnp.int32, (TK_SLC, QT), 0)
        causal = kpos <= qt * QT + lax.broadcasted_iota(jnp.int32, (TK_SLC, QT), 1)
        nb_t = TK_SLC // SLC_BLOCK
        parts = []
        for gi in range(2):
            rows = sel_ref[gi, pl.ds(pl.multiple_of(kt * nb_t, nb_t), nb_t), :]
            mk = jnp.concatenate([jnp.broadcast_to(rows[r:r + 1], (SLC_BLOCK, QT)) for r in range(nb_t)],
                                 axis=0)
            parts += [jnp.where((mk > 0.5) & causal, 0.0, NEG)] * HEADS_PER_KV
        s = _dot(ks_ref[pl.ds(k0, TK_SLC), :].astype(BF16), qT) + jnp.concatenate(parts, axis=1)
        m_new = jnp.maximum(m_prev, jnp.max(s, axis=0, keepdims=True))
        alpha = jnp.exp2(m_prev - m_new)
        p = jnp.exp2(s - m_new)
        l_new = alpha * l_prev + jnp.sum(p, axis=0, keepdims=True)
        vt = _tile_t(vs_ref, k0, TK_SLC // LANES).astype(BF16)
        acc_ref[...] = alpha * acc_ref[...] + _dot(vt, p.astype(BF16))
        return m_new, l_new

    acc_ref[...] = jnp.zeros(acc_ref.shape, F32)
    n_slc_tiles = (qt * QT + QT + TK_SLC - 1) // TK_SLC
    _, l_s = lax.fori_loop(0, n_slc_tiles, slc_body,
                           (jnp.full((1, COLS), NEG, F32), jnp.zeros((1, COLS), F32)))
    o_s = acc_ref[...] / l_s

    w0 = pl.multiple_of(jnp.maximum(qt - WINDOW // QT, 0) * QT, QT)
    d = (qt * QT + lax.broadcasted_iota(jnp.int32, (WIN_KEYS, QT), 1)
         - (w0 + lax.broadcasted_iota(jnp.int32, (WIN_KEYS, QT), 0)))
    sw = (_dot(kw_ref[pl.ds(w0, WIN_KEYS), :].astype(BF16), qT)
          + tile8(jnp.where((d >= 0) & (d < WINDOW), 0.0, NEG)))
    pw = jnp.exp2(sw - jnp.max(sw, axis=0, keepdims=True))
    vwt = _tile_t(vw_ref, w0, WIN_KEYS // LANES).astype(BF16)
    o_w = _dot(vwt, pw.astype(BF16)) / jnp.sum(pw, axis=0, keepdims=True)

    gs_ref[...] = jax.nn.sigmoid(gt_ref[...] + bg_ref[...]).T

    def gate_row(c):
        g8 = gs_ref[pl.ds(pl.multiple_of(c * N_HEADS + gp * 8, 8), 8), :]
        return jnp.concatenate([g8[hh:hh + 1, :] for hh in range(8)], axis=1)

    comb = gate_row(0) * o_c + gate_row(1) * o_s + gate_row(2) * o_w
    for m in range(4):
        r0 = (m // 2) * HEAD_DIM
        top = comb[r0:r0 + HEAD_DIM, (2 * m) * QT:(2 * m + 1) * QT]
        bot = comb[r0:r0 + HEAD_DIM, (2 * m + 1) * QT:(2 * m + 2) * QT]
        o_ref[:, m * LANES:(m + 1) * LANES] = jnp.concatenate([top, bot], axis=0).T


def _attn_consts(seq):
    n = np.arange(LANES)
    ci = n[:, None] * CMP_STRIDE
    sj = n[None, :] * SLC_BLOCK
    n_cmp = (seq - CMP_BLOCK) // CMP_STRIDE + 1
    n_slc = -(-seq // SLC_BLOCK)
    ov = ((ci < sj + SLC_BLOCK) & (ci + CMP_BLOCK > sj) & (n[:, None] < n_cmp) & (n[None, :] < n_slc))
    return jnp.asarray(ov.T.astype(np.float32), BF16)


def _attn_prompt(q, kv4, kvw, kc, vc, gates, bgate, n_batch, seq):
    ovt = _attn_consts(seq)
    nqt = seq // QT
    kspec = lambda col0: pl.BlockSpec((seq, LANES), lambda b, gp, t: (b, col0 + gp))
    cspec = pl.BlockSpec((None, None, 128, LANES), lambda b, gp, t: (b, gp, 0, 0))
    return pl.pallas_call(
        _attn_kernel,
        grid=(n_batch, 2, nqt),
        in_specs=[
            pl.BlockSpec((QT, 512), lambda b, gp, t: (b * nqt + t, gp)),
            kspec(4), kspec(6), kspec(0), kspec(2), cspec, cspec,
            pl.BlockSpec((QT, LANES), lambda b, gp, t: (b * nqt + t, 0)),
            pl.BlockSpec((1, LANES), lambda b, gp, t: (0, 0)),
            pl.BlockSpec(ovt.shape, lambda b, gp, t: (0, 0)),
        ],
        out_specs=pl.BlockSpec((QT, 512), lambda b, gp, t: (b * nqt + t, gp)),
        out_shape=jax.ShapeDtypeStruct((n_batch * seq, 1024), F32),
        scratch_shapes=[pltpu.VMEM((LANES, COLS), F32), pltpu.VMEM((LANES, QT), F32),
                        pltpu.VMEM((2, 32, QT), F32)],
        compiler_params=_cparams(3),
        name="nsa_attn_prompt",
    )(q, kv4, kv4, kvw, kvw, kc, vc, gates, bgate, ovt)


N_PAGES = 16
BPS = 2
SLAB_PITCH = 24


def _s1_kernel(layer, pt_ref, cache_ref, q_ref, wcat_ref, p_ref, w2_ref, c_ref, s1_ref, s2_ref,
               ov_ref, oc_ref, sel_ref, raw_ref, slab_ref, lhs_ref, sem_ref):
    step = pl.program_id(0)
    n_steps = pl.num_programs(0)

    def page_copy(st, slot, bi, p):
        return pltpu.make_async_copy(
            cache_ref.at[layer, pt_ref[st * BPS + bi, p], pl.ds(0, 2)],
            raw_ref.at[slot, bi, p],
            sem_ref.at[slot])

    def start_all(st, slot):
        for bi in range(BPS):
            for p in range(N_PAGES):
                page_copy(st, slot, bi, p).start()

    slot = step % 2

    @pl.when(step == 0)
    def _():
        start_all(0, 0)

    @pl.when(step + 1 < n_steps)
    def _():
        start_all(step + 1, 1 - slot)

    for bi in range(BPS):
        for p in range(N_PAGES):
            page_copy(step, slot, bi, p).wait()

    for bi in range(BPS):
        _s1_one(bi, slot, q_ref, wcat_ref, p_ref, w2_ref, c_ref, s1_ref, s2_ref, ov_ref, oc_ref, sel_ref,
                raw_ref, slab_ref, lhs_ref)


def _s1_one(bi, slot, q_ref, wcat_ref, p_ref, w2_ref, c_ref, s1_ref, s2_ref, ov_ref, oc_ref, sel_ref,
            raw_ref, slab_ref, lhs_ref):
    chunks = PAGE_SIZE // CMP_STRIDE
    for p in range(N_PAGES):
        for cb in range(4):
            kind, pr = cb // 2, cb % 2
            t = raw_ref[slot, bi, p, kind, pr * LANES:(pr + 1) * LANES, :].T
            for m in range(chunks):
                r0 = (p * chunks + m) * SLAB_PITCH
                slab_ref[bi, cb, r0:r0 + CMP_STRIDE, :] = t[m * CMP_STRIDE:(m + 1) * CMP_STRIDE]

    def load(l, cb):
        return slab_ref[bi, cb, pl.ds(l, 128, stride=SLAB_PITCH), :]

    outs = []
    for kind in range(2):
        _fill_lhs_strided(load, lhs_ref.at[bi], kind)
        outs.append(_compress_mlp(lhs_ref.at[bi], wcat_ref, p_ref, w2_ref, kind))
    ko, vo = outs
    kc = jnp.concatenate([_rope_slab(k, c_ref[...], s1_ref[...], s2_ref[...]) for k in ko], axis=1)
    vc = jnp.concatenate(vo, axis=1)

    qp = (q_ref[bi] * SCALE).astype(BF16)
    sc = _dot_nt(qp, kc.astype(BF16))
    lane = lax.broadcasted_iota(jnp.int32, (N_HEADS, LANES), 1)
    ok = lane < (LANES - 1)
    sc = jnp.where(ok, sc, NEG)
    mc = jnp.max(sc, axis=1, keepdims=True)
    ec = jnp.where(ok, jnp.exp(sc - mc), 0.0)
    pc = ec / jnp.sum(ec, axis=1, keepdims=True)
    oc_ref[bi] = _dot(pc.astype(BF16), vc.astype(BF16))

    imp_h = _split_dot(pc, ov_ref[...])
    rowg = jnp.right_shift(lax.broadcasted_iota(jnp.int32, (N_HEADS, LANES), 0), 2)
    row8 = lax.broadcasted_iota(jnp.int32, (8, LANES), 0)
    imp = jnp.zeros((8, LANES), F32)
    for g in range(N_KV):
        ig = jnp.sum(jnp.where(rowg == g, imp_h, 0.0), axis=0, keepdims=True)
        imp = jnp.where(row8 == g, ig, imp)
    lane8 = lax.broadcasted_iota(jnp.int32, (8, LANES), 1)
    jf = lane8.astype(F32)
    last = 2048 // SLC_BLOCK
    score = jnp.where((lane8 >= 1) & (lane8 <= last - 2), imp, NEG)
    picks = jnp.where(lane8 == 6, float(last - 1), 0.0)
    for k in range(TOP_N - 3):
        mx = jnp.max(score, axis=1, keepdims=True)
        first = jnp.min(jnp.where(score == mx, jf, 1e9), axis=1, keepdims=True)
        picks = jnp.where(lane8 == k, first, picks)
        score = jnp.where(jf == first, -3e38, score)
    sel_ref[bi] = picks.astype(jnp.int32)


def _s1(layer, page_table, cache, q_exp, wcat, pflat, w2bd, ctabs, ov):
    nb = q_exp.shape[0]
    full = lambda a: pl.BlockSpec(a.shape, lambda b, pt: (0,) * a.ndim)
    grid_spec = pltpu.PrefetchScalarGridSpec(
        num_scalar_prefetch=1,
        grid=(nb // BPS,),
        in_specs=[pl.BlockSpec(memory_space=pl.ANY),
                  pl.BlockSpec((BPS, N_HEADS, 256), lambda b, pt: (b, 0, 0)),
                  full(wcat), full(pflat), full(w2bd), full(ctabs[0]), full(ctabs[1]), full(ctabs[2]),
                  full(ov)],
        out_specs=[pl.BlockSpec((BPS, N_HEADS, 256), lambda b, pt: (b, 0, 0)),
                   pl.BlockSpec((BPS, 8, LANES), lambda b, pt: (b, 0, 0))],
        scratch_shapes=[pltpu.VMEM((2, BPS, N_PAGES, 2, 256, PAGE_SIZE), F32),
                        pltpu.VMEM((BPS, 4, 128 * SLAB_PITCH, LANES), F32),
                        pltpu.VMEM((BPS, 512, 1024), BF16),
                        pltpu.SemaphoreType.DMA((2,))],
    )
    return pl.pallas_call(
        functools.partial(_s1_kernel, layer),
        grid_spec=grid_spec,
        out_shape=[jax.ShapeDtypeStruct((nb, N_HEADS, 256), F32),
                   jax.ShapeDtypeStruct((nb, 8, LANES), jnp.int32)],
        compiler_params=_cparams(1),
        name="nsa_sample_s1",
    )(page_table, cache, q_exp, wcat, pflat, w2bd, *ctabs, ov)


N_HIST = TOP_N - 1
KSEL = N_HIST * PAGE_SIZE


def _s2_kernel(layer, has_prev, pt_ref, sel_ref, cache_ref, win_ref, q_ref, oc_ref, kvn_ref, kwn_ref,
               gt_ref, bg_ref, hsel_ref, *rest):
    o_ref, nwin_ref, kbuf_ref, vbuf_ref, sem_ref = rest[1:] if has_prev else rest
    if has_prev:
        nwin_ref[0:layer] = rest[0][...]
    step = pl.program_id(0)
    n_steps = pl.num_programs(0)

    def copies(st, slot, bi, g, k):
        bb = st * BPS + bi
        blk = sel_ref[(bb * N_KV + g) * 8 + k]
        page = pt_ref[bb, jnp.right_shift(blk, 1)]
        rows = pl.ds(g * HEAD_DIM, HEAD_DIM)
        dst = pl.ds(k * PAGE_SIZE, PAGE_SIZE)
        return (pltpu.make_async_copy(cache_ref.at[layer, page, 2, rows, :],
                                      kbuf_ref.at[slot, bi, g, :, dst], sem_ref.at[slot]),
                pltpu.make_async_copy(cache_ref.at[layer, page, 3, rows, :],
                                      vbuf_ref.at[slot, bi, g, :, dst], sem_ref.at[slot]))

    def for_all(st, slot, fn):
        for bi in range(BPS):
            for g in range(N_KV):
                for k in range(N_HIST):
                    ck, cv = copies(st, slot, bi, g, k)
                    fn(ck)
                    fn(cv)

    slot = step % 2

    @pl.when(step == 0)
    def _():
        for_all(0, 0, lambda c: c.start())

    @pl.when(step + 1 < n_steps)
    def _():
        for_all(step + 1, 1 - slot, lambda c: c.start())

    for_all(step, slot, lambda c: c.wait())

    for bi in range(BPS):
        _s2_one(layer, bi, step * BPS + bi, slot, sel_ref, win_ref, q_ref, oc_ref, kvn_ref, kwn_ref,
                gt_ref, bg_ref, hsel_ref, o_ref, nwin_ref, kbuf_ref, vbuf_ref)


def _s2_one(layer, bi, b, slot, sel_ref, win_ref, q_ref, oc_ref, kvn_ref, kwn_ref, gt_ref, bg_ref,
            hsel_ref, o_ref, nwin_ref, kbuf_ref, vbuf_ref):
    qf = q_ref[bi] * SCALE
    qp = qf.astype(BF16)
    q16 = (qf[:, 0:64] + qf[:, 64:128] + qf[:, 128:192] + qf[:, 192:256]).astype(BF16)
    lane_k = lax.broadcasted_iota(jnp.int32, (N_HEADS, KSEL), 1)
    rowg_k = jnp.right_shift(lax.broadcasted_iota(jnp.int32, (N_HEADS, KSEL), 0), 2)
    rowg_d = jnp.right_shift(lax.broadcasted_iota(jnp.int32, (N_HEADS, HEAD_DIM), 0), 2)
    tile_k = jnp.right_shift(lane_k, 7)
    half_k = jnp.bitwise_and(jnp.right_shift(lane_k, 6), 1)

    def attend(s_hist, s_new, pv_hist, v_new):
        m = jnp.maximum(jnp.max(s_hist, axis=1, keepdims=True), s_new)
        e = jnp.exp(s_hist - m)
        en = jnp.exp(s_new - m)
        l = jnp.sum(e, axis=1, keepdims=True) + en
        return (pv_hist(e.astype(BF16)) + en * v_new) / l

    kn = kvn_ref[bi]
    s_hist = jnp.zeros((N_HEADS, KSEL), F32)
    v16 = jnp.zeros((N_HEADS, HEAD_DIM), F32)
    for g in range(N_KV):
        sg = _dot(q16, kbuf_ref[slot, bi, g].astype(BF16))
        want = jnp.zeros((N_HEADS, KSEL), jnp.int32)
        for k in range(N_HIST):
            half = jnp.bitwise_and(sel_ref[(b * N_KV + g) * 8 + k], 1)
            want = jnp.where(tile_k == k, half, want)
        s_hist = jnp.where(rowg_k == g, jnp.where(half_k == want, sg, NEG), s_hist)
        v16 = jnp.where(rowg_d == g, kn[:, 768 + g * HEAD_DIM:768 + (g + 1) * HEAD_DIM], v16)
    s_new = jnp.sum(qf * kn[:, 512:768], axis=1, keepdims=True)

    def pv_slc(e):
        o = jnp.zeros((N_HEADS, HEAD_DIM), F32)
        for g in range(N_KV):
            o = jnp.where(rowg_d == g, _dot_nt(e, vbuf_ref[slot, bi, g].astype(BF16)), o)
        return o

    o_s = attend(s_hist, s_new, pv_slc, v16)
    o_s = jnp.concatenate([o_s] * N_KV, axis=1)

    wn = kwn_ref[bi]
    s_w = _dot(qp, win_ref[bi, 0].astype(BF16))
    colw = lax.broadcasted_iota(jnp.int32, (N_HEADS, WINDOW), 1)
    s_w = jnp.where(colw >= 1, s_w, NEG)
    s_wn = jnp.sum(qf * wn[:, 0:256], axis=1, keepdims=True)
    vw = win_ref[bi, 1].astype(BF16)
    o_w = attend(s_w, s_wn, lambda e: _dot_nt(e, vw), wn[:, 256:512])

    lane = lax.broadcasted_iota(jnp.int32, (LANES, LANES), 1)
    for c in range(2):
        for rs in range(2):
            rows = slice(rs * LANES, (rs + 1) * LANES)
            new = wn[:, c * 256 + rs * LANES:c * 256 + (rs + 1) * LANES]
            col = jnp.broadcast_to(new, (LANES, LANES)).T
            tiles = [pltpu.roll(win_ref[bi, c, rows, j * LANES:(j + 1) * LANES], LANES - 1, 1)
                     for j in range(WINDOW // LANES)]
            tiles.append(col)
            for j in range(WINDOW // LANES):
                nwin_ref[layer, bi, c, rows, j * LANES:(j + 1) * LANES] = jnp.where(
                    lane == LANES - 1, tiles[j + 1], tiles[j])

    gsig = jax.nn.sigmoid(gt_ref[bi] + bg_ref[...])
    gcol = lambda c: jnp.sum(hsel_ref[c] * gsig, axis=1, keepdims=True)
    o_ref[bi] = gcol(0) * oc_ref[bi] + gcol(1) * o_s + gcol(2) * o_w


def _s2(layer, page_table, sel_flat, cache, win, q_exp, oc, kv4s, kvws, gates, bgate, hsel, prev):
    nb = q_exp.shape[0]
    hq = pl.BlockSpec((BPS, N_HEADS, 256), lambda b, pt, sl: (b, 0, 0))
    row = lambda w: pl.BlockSpec((BPS, 1, w), lambda b, pt, sl: (b, 0, 0))
    wspec = pl.BlockSpec((None, BPS, 2, 256, WINDOW), lambda b, pt, sl: (layer, b, 0, 0, 0))
    stack = lambda n: pl.BlockSpec((n, BPS, 2, 256, WINDOW), lambda b, pt, sl: (0, b, 0, 0, 0))
    in_specs = [pl.BlockSpec(memory_space=pl.ANY), wspec,
                hq, hq, row(1024), row(512), row(LANES),
                pl.BlockSpec((1, LANES), lambda b, pt, sl: (0, 0)),
                pl.BlockSpec(hsel.shape, lambda b, pt, sl: (0, 0, 0))]
    args = [page_table, sel_flat, cache, win, q_exp, oc, kv4s, kvws, gates, bgate, hsel]
    if prev is not None:
        in_specs.append(stack(layer))
        args.append(prev)
    grid_spec = pltpu.PrefetchScalarGridSpec(
        num_scalar_prefetch=2,
        grid=(nb // BPS,),
        in_specs=in_specs,
        out_specs=[hq, stack(layer + 1)],
        scratch_shapes=[pltpu.VMEM((2, BPS, N_KV, HEAD_DIM, KSEL), F32),
                        pltpu.VMEM((2, BPS, N_KV, HEAD_DIM, KSEL), F32),
                        pltpu.SemaphoreType.DMA((2,))],
    )
    return pl.pallas_call(
        functools.partial(_s2_kernel, layer, prev is not None),
        grid_spec=grid_spec,
        out_shape=[jax.ShapeDtypeStruct((nb, N_HEADS, 256), F32),
                   jax.ShapeDtypeStruct((layer + 1, nb, 2, 256, WINDOW), F32)],
        compiler_params=_cparams(1),
        name="nsa_sample_s2",
    )(*args)


def _sg_in_kernel(x_ref, g_ref, w_ref, lg_ref, lb_ref, u_ref, v_ref):
    xb = _rmsnorm(x_ref[...], g_ref[...]).astype(BF16)
    u_ref[...] = jax.nn.gelu(_dot(xb, w_ref[:, 0:1024]))
    v = jax.nn.gelu(_dot(xb, w_ref[:, 1024:2048]))
    mu = jnp.mean(v, axis=-1, keepdims=True)
    var = jnp.mean(jnp.square(v - mu), axis=-1, keepdims=True)
    v_ref[...] = (v - mu) * lax.rsqrt(var + EPS) * lg_ref[...] + lb_ref[...]


def _sg_in(x, g, w, lg, lb, tm):
    m = x.shape[0]
    full = lambda a: pl.BlockSpec(a.shape, lambda i: (0,) * a.ndim, pipeline_mode=pl.Buffered(1))
    row = pl.BlockSpec((tm, 1024), lambda i: (i, 0))
    return pl.pallas_call(
        _sg_in_kernel,
        grid=(m // tm,),
        in_specs=[row, full(g), full(w), full(lg), full(lb)],
        out_specs=[row, row],
        out_shape=[jax.ShapeDtypeStruct((m, 1024), F32)] * 2,
        compiler_params=_cparams(1),
        name="sg_in",
    )(x, g, w, lg, lb)


FF_CHUNK = 1024


def _ffn_tail(final, x1, g_ref, w1_ref, w2_ref, gfin_ref, o_ref):
    xb = _rmsnorm(x1, g_ref[...]).astype(BF16)
    acc = x1
    for c in range(D_FF // FF_CHUNK):
        sl = slice(c * FF_CHUNK, (c + 1) * FF_CHUNK)
        h = jnp.maximum(_dot(xb, w1_ref[:, sl]), 0.0)
        acc = acc + _dot((h * h).astype(BF16), w2_ref[sl, :])
    o_ref[...] = _rmsnorm(acc, gfin_ref[...]) if final else acc


def _post_kernel(final, x_ref, a_ref, wo_ref, g_ref, w1_ref, w2_ref, gfin_ref, o_ref):
    x1 = x_ref[...] + _dot(a_ref[...].astype(BF16), wo_ref[...])
    _ffn_tail(final, x1, g_ref, w1_ref, w2_ref, gfin_ref, o_ref)


def _post_sg_prompt_kernel(final, x_ref, u_ref, v_ref, ws_ref, bs_ref, wo_ref, g_ref, w1_ref, w2_ref,
                           gfin_ref, o_ref, a_ref):
    r = lax.broadcasted_iota(jnp.int32, (CHUNK, CHUNK), 0)
    c = lax.broadcasted_iota(jnp.int32, (CHUNK, CHUNK), 1)
    tril = r >= c
    for g in range(SG_GROUPS):
        w = jnp.where(tril, ws_ref[g], 0.0).astype(BF16)
        gl = slice(g * 128, (g + 1) * 128)
        for ch in range(x_ref.shape[0] // CHUNK):
            rs = slice(ch * CHUNK, (ch + 1) * CHUNK)
            s = _dot(w, v_ref[rs, gl].astype(BF16)) + bs_ref[:, gl]
            a_ref[rs, gl] = (u_ref[rs, gl] * s).astype(BF16)
    x1 = x_ref[...] + _dot(a_ref[...], wo_ref[...])
    _ffn_tail(final, x1, g_ref, w1_ref, w2_ref, gfin_ref, o_ref)


def _post_sg_sample_kernel(final, x_ref, u_ref, v_ref, w0_ref, b0_ref, wo_ref, g_ref, w1_ref, w2_ref,
                           gfin_ref, o_ref):
    a = u_ref[...] * (v_ref[...] * w0_ref[...] + b0_ref[...])
    x1 = x_ref[...] + _dot(a.astype(BF16), wo_ref[...])
    _ffn_tail(final, x1, g_ref, w1_ref, w2_ref, gfin_ref, o_ref)


def _post_call(kern, name, final, x, row_ins, full_ins, tm, scratch=()):
    m = x.shape[0]
    full = lambda a: pl.BlockSpec(a.shape, lambda i: (0,) * a.ndim, pipeline_mode=pl.Buffered(1))
    row = lambda a: pl.BlockSpec((tm, a.shape[1]), lambda i: (i, 0))
    return pl.pallas_call(
        functools.partial(kern, final),
        grid=(m // tm,),
        in_specs=[row(a) for a in row_ins] + [full(a) for a in full_ins],
        out_specs=pl.BlockSpec((tm, D_MODEL), lambda i: (i, 0)),
        out_shape=jax.ShapeDtypeStruct((m, D_MODEL), F32),
        scratch_shapes=list(scratch),
        compiler_params=_cparams(1),
        name=name,
    )(*row_ins, *full_ins)


def _expand_heads(w, axis):
    parts = []
    for h in range(N_HEADS):
        s = h // HEADS_PER_KV
        pad = [(0, 0), (0, 0)]
        pad[axis] = (s * HEAD_DIM, (N_KV - 1 - s) * HEAD_DIM)
        parts.append(jnp.pad(lax.slice_in_dim(w, h * HEAD_DIM, (h + 1) * HEAD_DIM, axis=axis), pad))
    return jnp.concatenate(parts, axis=axis)


def kernel(x_prompt, x_sample, cache_nsa_kv, state_nsa_win, page_table, g_mix, g_ffn, g_final,
           nsa_w_in, nsa_b_gate, nsa_cmp_pos, nsa_cmp_w1, nsa_cmp_w2, nsa_w_out,
           sg_w_in, sg_ln_g, sg_ln_b, sg_w_spatial, sg_b_spatial, sg_w_out, ffn_w1, ffn_w2):
    nb, seq, _ = x_prompt.shape
    nd = x_sample.shape[0]
    depth = g_mix.shape[0]
    n_pool = cache_nsa_kv.shape[1]
    past = page_table.shape[1] * PAGE_SIZE
    tm_p = 512

    xp = x_prompt.reshape(nb * seq, D_MODEL)
    xs = x_sample.reshape(nd, D_MODEL)
    cache = cache_nsa_kv.transpose(0, 1, 3, 4, 5, 2).reshape(cache_nsa_kv.shape[0], n_pool, 4, 256,
                                                              PAGE_SIZE)
    win = state_nsa_win.transpose(0, 1, 3, 4, 5, 2).reshape(state_nsa_win.shape[0], nd, 2, 256, WINDOW)
    nwin = None

    tabs_p = _rope_tables(jnp.arange(seq, dtype=jnp.int32))
    tabs_s = _rope_tables(jnp.full((nd,), past, jnp.int32))
    tabs_c = _rope_tables(jnp.arange(LANES, dtype=jnp.int32) * CMP_STRIDE + CMP_BLOCK - 1)

    nidx = np.arange(LANES)
    ci = nidx[:, None] * CMP_STRIDE
    sj = nidx[None, :] * SLC_BLOCK
    ov_s = ((ci < sj + SLC_BLOCK) & (ci + CMP_BLOCK > sj) & (nidx[:, None] < LANES - 1))
    ov_s = jnp.asarray(ov_s.astype(np.float32), BF16)
    hsel = np.zeros((3, N_HEADS, LANES), np.float32)
    for c in range(3):
        hsel[c, np.arange(N_HEADS), c * N_HEADS + np.arange(N_HEADS)] = 1.0
    hsel = jnp.asarray(hsel)

    gfin = g_final.reshape(1, D_MODEL)
    kvt, kwt = None, None
    kv_s, v_s = [], []
    for i in range(depth):
        j = i // 2
        final = i == depth - 1
        gm = g_mix[i].reshape(1, D_MODEL)
        gf = g_ffn[i].reshape(1, D_MODEL)
        w1 = ffn_w1[i].astype(BF16)
        w2 = ffn_w2[i].astype(BF16)
        if i % 2 == 0:
            w_in = nsa_w_in[j]
            wq = w_in[:, 0:1024].astype(BF16)
            wq_exp = _expand_heads(wq, 1)
            wkv = w_in[:, 1024:2048].astype(BF16)
            wkw = w_in[:, 2048:2560].astype(BF16)
            wg = jnp.pad(w_in[:, 2560:2608], ((0, 0), (0, LANES - 48))).astype(BF16)
            bgate = jnp.pad(nsa_b_gate[j], (0, LANES - 48)).reshape(1, LANES)
            w1c = nsa_cmp_w1[j]
            wcat = jnp.concatenate([w1c[:, :16].reshape(2, 1024, CMP_HID),
                                    w1c[:, 16:].reshape(2, 1024, CMP_HID)], axis=2).astype(BF16)
            pe = nsa_cmp_pos[j]
            pflat = jnp.concatenate([pe[:, :16].reshape(2, 1, 1024), pe[:, 16:].reshape(2, 1, 1024),
                                     jnp.zeros((2, 6, 1024), F32)], axis=1).astype(BF16)
            w2c = nsa_cmp_w2[j]
            zc = jnp.zeros_like(w2c)
            w2bd = jnp.concatenate([jnp.concatenate([w2c, zc], axis=2),
                                    jnp.concatenate([zc, w2c], axis=2)], axis=1).astype(BF16)
            wo = nsa_w_out[j].astype(BF16)
            wo_exp = _expand_heads(wo, 0)

            q_p, kv4_p, kvw_p, gt_p, kvt, kwt = _proj(xp, gm, tabs_p, seq // tm_p, wq, wkv, wkw, wg, tm_p,
                                                      stack=(j, nb, seq, kvt, kwt))
            q_s, kv4_s, kvw_s, gt_s = _proj(xs, gm, tabs_s, 1, wq_exp, wkv, wkw, wg, nd)

            kc, vc = _cmp_prompt(kv4_p, nb, seq, wcat, pflat, w2bd, tabs_c)
            a_p = _attn_prompt(q_p, kv4_p, kvw_p, kc, vc, gt_p, bgate, nb, seq)

            q_exp = q_s.reshape(nd, N_HEADS, 256)
            oc_s, sel = _s1(j, page_table, cache, q_exp, wcat, pflat, w2bd, tabs_c, ov_s)
            sel_flat = sel[:, :N_KV, :8].reshape(-1)
            a_s, nwin = _s2(j, page_table, sel_flat, cache, win, q_exp, oc_s,
                            kv4_s.reshape(nd, 1, 1024), kvw_s.reshape(nd, 1, 512),
                            gt_s.reshape(nd, 1, LANES), bgate, hsel, nwin)

            xp = _post_call(_post_kernel, "post_nsa", final, xp, [xp, a_p], [wo, gf, w1, w2, gfin], tm_p)
            xs = _post_call(_post_kernel, "post_nsa", final, xs, [xs, a_s.reshape(nd, 4096)],
                            [wo_exp, gf, w1, w2, gfin], nd)
            kv_s.append(kv4_s.reshape(nd, 1, 4, N_KV, HEAD_DIM))
        else:
            w_in = sg_w_in[j].astype(BF16)
            lg = sg_ln_g[j].reshape(1, 1024)
            lb = sg_ln_b[j].reshape(1, 1024)
            ws = sg_w_spatial[j]
            bs = sg_b_spatial[j]
            bs_exp = jnp.repeat(bs.T, CHUNK, axis=1)
            w0 = jnp.repeat(ws[:, 0, 0], CHUNK).reshape(1, 1024)
            b0 = bs_exp[0:1]
            wo = sg_w_out[j].astype(BF16)

            u_p, v_p = _sg_in(xp, gm, w_in, lg, lb, tm_p)
            u_s, vv_s = _sg_in(xs, gm, w_in, lg, lb, nd)
            xp = _post_call(_post_sg_prompt_kernel, "post_sg_prompt", final, xp, [xp, u_p, v_p],
                            [ws, bs_exp, wo, gf, w1, w2, gfin], tm_p,
                            scratch=[pltpu.VMEM((tm_p, 1024), BF16)])
            xs = _post_call(_post_sg_sample_kernel, "post_sg_sample", final, xs, [xs, u_s, vv_s],
                            [w0, b0, wo, gf, w1, w2, gfin], nd)
            v_s.append(vv_s.reshape(nd, 1, 1024))

    y_prompt = xp.reshape(nb, seq, D_MODEL)
    y_sample = xs.reshape(nd, 1, D_MODEL)
    n_l = kvt.shape[0]
    to_rows = lambda a, kinds, rows: a.reshape(n_l, a.shape[1], kinds, N_KV, HEAD_DIM, rows).transpose(
        0, 1, 5, 2, 3, 4)
    return (y_prompt, y_sample, to_rows(kvt, 4, seq), to_rows(kwt, 2, WINDOW), jnp.stack(kv_s),
            to_rows(nwin, 2, WINDOW), jnp.stack(v_s))
```
